```python
import jax, jax.numpy as jnp
from jax import lax
import numpy as np

D_MODEL = 1024
BATCH = 16
SEQ = 4096
DEPTH = 1

ATTN_HEADS = 8
KV_HEADS = 2
HEAD_DIM = 64
ATTN_WIDTH = ATTN_HEADS * HEAD_DIM
KV_WIDTH = KV_HEADS * HEAD_DIM
GROUP = ATTN_HEADS // KV_HEADS
WINDOW = 128
BLOCK = 128
CONV_WIDTH = D_MODEL - ATTN_WIDTH
CONV_K = 3
D_FF = 4 * D_MODEL
EPS = 1e-5
IN_WIDTH = ATTN_WIDTH + 2 * KV_WIDTH + 3 * CONV_WIDTH
SPLITS = (ATTN_WIDTH,
          ATTN_WIDTH + KV_WIDTH,
          ATTN_WIDTH + 2 * KV_WIDTH,
          ATTN_WIDTH + 2 * KV_WIDTH + CONV_WIDTH,
          ATTN_WIDTH + 2 * KV_WIDTH + 2 * CONV_WIDTH)

kernel_name = "hymba_swa_sink_alibi_shortconv_relu2"


def _rmsnorm(x, g):
    xf = x.astype(jnp.float32)
    inv = lax.rsqrt(jnp.mean(xf * xf, axis=-1, keepdims=True) + EPS)
    return (xf * inv * g.astype(jnp.float32)).astype(x.dtype)


def _alibi_slopes(n):
    return jnp.exp2(-8.0 * (jnp.arange(n, dtype=jnp.float32) + 1.0) / n)


def _window_attention(q, k, v, sinks):
    b, s, _ = q.shape
    nb = s // BLOCK
    qb = q.reshape(b, nb, BLOCK, KV_HEADS, GROUP, HEAD_DIM)
    kb = k.reshape(b, nb, BLOCK, KV_HEADS, HEAD_DIM)
    vb = v.reshape(b, nb, BLOCK, KV_HEADS, HEAD_DIM)
    pad = ((0, 0), (1, 0), (0, 0), (0, 0), (0, 0))
    k_band = jnp.concatenate([jnp.pad(kb, pad)[:, :-1], kb], axis=2)
    v_band = jnp.concatenate([jnp.pad(vb, pad)[:, :-1], vb], axis=2)

    scale = HEAD_DIM ** -0.5
    scores = jnp.einsum('bnqkgd,bnskd->bkgnqs', qb, k_band,
                        preferred_element_type=jnp.float32) * scale

    qi = jnp.arange(BLOCK)[:, None]
    kj = jnp.arange(2 * BLOCK)[None, :]
    dist = qi - kj + BLOCK
    blk = jnp.arange(nb)[:, None, None]
    key_pos = blk * BLOCK - BLOCK + kj[None]
    mask = (dist[None] >= 0) & (dist[None] < WINDOW) & (key_pos >= 0)

    slopes = _alibi_slopes(ATTN_HEADS).reshape(KV_HEADS, GROUP)
    bias = -slopes[:, :, None, None, None] * dist.astype(jnp.float32)
    logits = jnp.where(mask, scores + bias, -jnp.inf)

    sink = sinks.astype(jnp.float32).reshape(KV_HEADS, GROUP)[None, :, :, None, None, None]
    m = jnp.maximum(jnp.max(logits, axis=-1, keepdims=True), sink)
    p = jnp.exp(logits - m)
    denom = jnp.sum(p, axis=-1, keepdims=True) + jnp.exp(sink - m)
    probs = (p / denom).astype(v.dtype)
    out = jnp.einsum('bkgnqs,bnskd->bnqkgd', probs, v_band)
    return out.reshape(b, s, ATTN_WIDTH)


def _short_conv(u, w):
    c = u.shape[-1]
    return lax.conv_general_dilated(
        u, w[:, None, :].astype(u.dtype), window_strides=(1,),
        padding=[(CONV_K - 1, 0)], dimension_numbers=('NWC', 'WIO', 'NWC'),
        feature_group_count=c)


def setup_inputs(seed: int = 0) -> dict:
    key = jax.random.key(seed)
    ks = jax.random.split(key, 13)
    f32 = jnp.float32

    def gain(k, n):
        return 1.0 + 0.02 * jax.random.normal(k, (DEPTH, n), f32)

    x = jax.random.normal(ks[0], (BATCH, SEQ, D_MODEL), f32)
    norm1_g = gain(ks[1], D_MODEL)
    w_in = jax.random.normal(ks[2], (DEPTH, D_MODEL, IN_WIDTH), f32) * D_MODEL ** -0.5
    conv_w = jax.random.normal(ks[3], (DEPTH, CONV_K, CONV_WIDTH), f32) * CONV_K ** -0.5
    sinks = jax.random.normal(ks[4], (DEPTH, ATTN_HEADS), f32)
    attn_norm_g = gain(ks[5], ATTN_WIDTH)
    conv_norm_g = gain(ks[6], CONV_WIDTH)
    w_out = jax.random.normal(ks[7], (DEPTH, D_MODEL, D_MODEL), f32) * D_MODEL ** -0.5
    norm2_g = gain(ks[8], D_MODEL)
    w_ff1 = jax.random.normal(ks[9], (DEPTH, D_MODEL, D_FF), f32) * D_MODEL ** -0.5
    w_ff2 = jax.random.normal(ks[10], (DEPTH, D_FF, D_MODEL), f32) * D_FF ** -0.5
    final_g = 1.0 + 0.02 * jax.random.normal(ks[11], (D_MODEL,), f32)
    return {"x": x, "norm1_g": norm1_g, "w_in": w_in, "conv_w": conv_w,
            "sinks": sinks, "attn_norm_g": attn_norm_g, "conv_norm_g": conv_norm_g,
            "w_out": w_out, "norm2_g": norm2_g, "w_ff1": w_ff1, "w_ff2": w_ff2,
            "final_g": final_g}


def reference(x, norm1_g, w_in, conv_w, sinks, attn_norm_g, conv_norm_g,
              w_out, norm2_g, w_ff1, w_ff2, final_g):
    h = x
    for l in range(DEPTH):
        y = _rmsnorm(h, norm1_g[l])
        proj = jnp.einsum('bsd,de->bse', y, w_in[l])
        q, k, v, c_gate, b_gate, u = jnp.split(proj, SPLITS, axis=-1)
        attn = _window_attention(q, k, v, sinks[l])
        conv = b_gate * _short_conv(c_gate * u, conv_w[l])
        mixed = jnp.concatenate([_rmsnorm(attn, attn_norm_g[l]),
                                 _rmsnorm(conv, conv_norm_g[l])], axis=-1)
        h = h + jnp.einsum('bse,ed->bsd', mixed, w_out[l])
        z = _rmsnorm(h, norm2_g[l])
        a = jnp.square(jax.nn.relu(jnp.einsum('bsd,df->bsf', z, w_ff1[l])))
        h = h + jnp.einsum('bsf,fd->bsd', a, w_ff2[l])
    return _rmsnorm(h, final_g)
```

```python
import functools

import jax
import jax.numpy as jnp
from jax import lax
from jax.experimental import pallas as pl
from jax.experimental.pallas import tpu as pltpu

D_MODEL = 1024
ATTN_HEADS = 8
KV_HEADS = 2
HEAD_DIM = 64
ATTN_WIDTH = ATTN_HEADS * HEAD_DIM
KV_WIDTH = KV_HEADS * HEAD_DIM
GROUP = ATTN_HEADS // KV_HEADS
WINDOW = 128
BLOCK = 128
CONV_WIDTH = D_MODEL - ATTN_WIDTH
CONV_K = 3
D_FF = 4 * D_MODEL
EPS = 1e-5
OFF_Q = 0
OFF_KV = ATTN_WIDTH
OFF_C = ATTN_WIDTH + 2 * KV_WIDTH
OFF_B = OFF_C + CONV_WIDTH
OFF_U = OFF_B + CONV_WIDTH
IN_WIDTH = OFF_U + CONV_WIDTH

LANES = 128
SUBLANES = 8
TILE = 512
N_BLOCKS = TILE // BLOCK
FF_CHUNK = 1024
MASKED = -1e30
VMEM_LIMIT_BYTES = 56 * 1024 * 1024


def _rms(xf, g):
    inv = lax.rsqrt(jnp.mean(xf * xf, axis=-1, keepdims=True) + EPS)
    return xf * inv * g


def _dot(a, b):
    return jnp.dot(a, b, preferred_element_type=jnp.float32)


def _layer_kernel(x_ref, g1_ref, w_in_ref, convw_ref, sinks_ref, ga_ref, gc_ref,
                  w_out_ref, g2_ref, w1_ref, w2_ref, gf_ref,
                  o_ref,
                  bias_scr, q_scr, k_scr, v_scr, cu_scr, attn_scr, mixed_scr):
    t = pl.program_id(1)
    bf16 = jnp.bfloat16
    f32 = jnp.float32

    @pl.when(t == 0)
    def _start_of_sequence():
        k_scr[:, 0:BLOCK, :] = jnp.zeros((4, BLOCK, LANES), bf16)
        v_scr[:, 0:BLOCK, :] = jnp.zeros((4, BLOCK, LANES), bf16)
        cu_scr[0:SUBLANES, :] = jnp.zeros((SUBLANES, CONV_WIDTH), f32)
        qi = lax.broadcasted_iota(jnp.int32, (BLOCK, 2 * BLOCK), 0)
        kj = lax.broadcasted_iota(jnp.int32, (BLOCK, 2 * BLOCK), 1)
        dist = qi - kj + BLOCK
        in_window = (dist >= 0) & (dist < WINDOW)
        distf = dist.astype(f32)
        for h in range(ATTN_HEADS):
            slope = 2.0 ** (-8.0 * (h + 1) / ATTN_HEADS)
            b = jnp.where(in_window, -slope * distf, MASKED)
            bias_scr[0, h] = b
            bias_scr[1, h] = jnp.where(kj >= BLOCK, b, MASKED)

    x = x_ref[...]
    y = _rms(x, g1_ref[...]).astype(bf16)

    q = _dot(y, w_in_ref[:, OFF_Q:OFF_Q + ATTN_WIDTH]) * (HEAD_DIM ** -0.5)
    q_scr[...] = q.astype(bf16)

    kv = _dot(y, w_in_ref[:, OFF_KV:OFF_KV + 2 * KV_WIDTH])
    low = lax.broadcasted_iota(jnp.int32, (TILE, LANES), 1) < HEAD_DIM
    zero = jnp.zeros((TILE, LANES), f32)
    for src, dst in ((kv[:, :KV_WIDTH], k_scr), (kv[:, KV_WIDTH:], v_scr)):
        swapped = pltpu.roll(src, HEAD_DIM, axis=1)
        dst[0, BLOCK:, :] = jnp.where(low, src, zero).astype(bf16)
        dst[1, BLOCK:, :] = jnp.where(low, zero, swapped).astype(bf16)
        dst[2, BLOCK:, :] = jnp.where(low, swapped, zero).astype(bf16)
        dst[3, BLOCK:, :] = jnp.where(low, zero, src).astype(bf16)

    cu = (_dot(y, w_in_ref[:, OFF_C:OFF_C + CONV_WIDTH])
          * _dot(y, w_in_ref[:, OFF_U:OFF_U + CONV_WIDTH]))
    cu_scr[SUBLANES:, :] = cu
    cw = convw_ref[...]
    conv = (cw[0:1, :] * cu_scr[SUBLANES - 2:SUBLANES - 2 + TILE, :]
            + cw[1:2, :] * cu_scr[SUBLANES - 1:SUBLANES - 1 + TILE, :]
            + cw[2:3, :] * cu)
    conv = _dot(y, w_in_ref[:, OFF_B:OFF_B + CONV_WIDTH]) * conv
    mixed_scr[:, ATTN_WIDTH:] = _rms(conv, gc_ref[...]).astype(bf16)

    first = jnp.where(t == 0, 1, 0)
    low_b = lax.broadcasted_iota(jnp.int32, (BLOCK, LANES), 1) < HEAD_DIM

    def _attend(j, carry):
        r0 = pl.multiple_of(j * BLOCK, BLOCK)
        table = jnp.where(j == 0, first, 0)
        for pair in range(ATTN_HEADS // 2):
            g = pair // (GROUP // 2)
            qp = q_scr[pl.ds(r0, BLOCK), pair * LANES:(pair + 1) * LANES]
            acc = None
            inv = []
            for r in range(2):
                h = 2 * pair + r
                kb = k_scr[2 * g + r, pl.ds(r0, 2 * BLOCK), :]
                vb = v_scr[2 * g + r, pl.ds(r0, 2 * BLOCK), :]
                s = lax.dot_general(qp, kb, (((1,), (1,)), ((), ())),
                                    preferred_element_type=f32)
                logits = s + bias_scr[table, h]
                sink = sinks_ref[h]
                m = jnp.maximum(jnp.max(logits, axis=-1, keepdims=True), sink)
                p = jnp.exp(logits - m)
                den = jnp.sum(p, axis=-1, keepdims=True) + jnp.exp(sink - m)
                inv.append(1.0 / den)
                pv = _dot(p.astype(bf16), vb)
                acc = pv if acc is None else acc + pv
            attn_scr[pl.ds(r0, BLOCK), pair * LANES:(pair + 1) * LANES] = (
                acc * jnp.where(low_b, inv[0], inv[1]))
        return carry

    lax.fori_loop(0, N_BLOCKS, _attend, 0)

    mixed_scr[:, :ATTN_WIDTH] = _rms(attn_scr[...], ga_ref[...]).astype(bf16)

    h1 = x + _dot(mixed_scr[...], w_out_ref[...])

    z = _rms(h1, g2_ref[...]).astype(bf16)
    h2 = h1
    for c in range(D_FF // FF_CHUNK):
        a = _dot(z, w1_ref[:, c * FF_CHUNK:(c + 1) * FF_CHUNK])
        a = jnp.square(jnp.maximum(a, 0.0)).astype(bf16)
        h2 = h2 + _dot(a, w2_ref[c * FF_CHUNK:(c + 1) * FF_CHUNK, :])

    o_ref[...] = _rms(h2, gf_ref[...])

    k_scr[:, 0:BLOCK, :] = k_scr[:, TILE:TILE + BLOCK, :]
    v_scr[:, 0:BLOCK, :] = v_scr[:, TILE:TILE + BLOCK, :]
    cu_scr[0:SUBLANES, :] = cu_scr[TILE:TILE + SUBLANES, :]


def _resident(shape):
    return pl.BlockSpec(shape, lambda b, t: (0,) * len(shape),
                        pipeline_mode=pl.Buffered(1))


@jax.jit
def kernel(x, norm1_g, w_in, conv_w, sinks, attn_norm_g, conv_norm_g, w_out, norm2_g,
           w_ff1, w_ff2, final_g):
    batch, seq, d = x.shape
    assert d == D_MODEL and seq % TILE == 0
    assert norm1_g.shape[0] == 1, "single layer"
    bf16 = jnp.bfloat16
    row = lambda v: v.reshape(1, -1).astype(jnp.float32)

    tile_spec = pl.BlockSpec((None, TILE, D_MODEL), lambda b, t: (b, t, 0))
    in_specs = [
        tile_spec,
        _resident((1, D_MODEL)),
        _resident((D_MODEL, IN_WIDTH)),
        _resident((CONV_K, CONV_WIDTH)),
        pl.BlockSpec(memory_space=pltpu.SMEM),
        _resident((1, ATTN_WIDTH)),
        _resident((1, CONV_WIDTH)),
        _resident((D_MODEL, D_MODEL)),
        _resident((1, D_MODEL)),
        _resident((D_MODEL, D_FF)),
        _resident((D_FF, D_MODEL)),
        _resident((1, D_MODEL)),
    ]
    scratch_shapes = [
        pltpu.VMEM((2, ATTN_HEADS, BLOCK, 2 * BLOCK), jnp.float32),
        pltpu.VMEM((TILE, ATTN_WIDTH), bf16),
        pltpu.VMEM((4, BLOCK + TILE, LANES), bf16),
        pltpu.VMEM((4, BLOCK + TILE, LANES), bf16),
        pltpu.VMEM((SUBLANES + TILE, CONV_WIDTH), jnp.float32),
        pltpu.VMEM((TILE, ATTN_WIDTH), jnp.float32),
        pltpu.VMEM((TILE, D_MODEL), bf16),
    ]
    return pl.pallas_call(
        _layer_kernel,
        grid=(batch, seq // TILE),
        in_specs=in_specs,
        out_specs=tile_spec,
        out_shape=jax.ShapeDtypeStruct(x.shape, x.dtype),
        scratch_shapes=scratch_shapes,
        compiler_params=pltpu.CompilerParams(
            dimension_semantics=("arbitrary", "arbitrary"),
            vmem_limit_bytes=VMEM_LIMIT_BYTES),
        name="hybrid_layer",
    )(x, row(norm1_g[0]), w_in[0].astype(bf16), conv_w[0].astype(jnp.float32),
      sinks[0].astype(jnp.float32), row(attn_norm_g[0]), row(conv_norm_g[0]),
      w_out[0].astype(bf16), row(norm2_g[0]), w_ff1[0].astype(bf16),
      w_ff2[0].astype(bf16), row(final_g))
```

```python
import jax
import jax.numpy as jnp
from jax import lax
from jax.experimental import pallas as pl
from jax.experimental.pallas import tpu as pltpu

D_MODEL = 1024
ATTN_HEADS = 8
KV_HEADS = 2
HEAD_DIM = 64
ATTN_WIDTH = ATTN_HEADS * HEAD_DIM
KV_WIDTH = KV_HEADS * HEAD_DIM
GROUP = ATTN_HEADS // KV_HEADS
WINDOW = 128
BLOCK = 128
CONV_WIDTH = D_MODEL - ATTN_WIDTH
CONV_K = 3
D_FF = 4 * D_MODEL
EPS = 1e-5
OFF_Q = 0
OFF_KV = ATTN_WIDTH
OFF_C = ATTN_WIDTH + 2 * KV_WIDTH
OFF_B = OFF_C + CONV_WIDTH
OFF_U = OFF_B + CONV_WIDTH
IN_WIDTH = OFF_U + CONV_WIDTH

LANES = 128
SUBLANES = 8
TILE = 512
N_BLOCKS = TILE // BLOCK
FF_CHUNK = 1024
MASKED = -1e30
VMEM_LIMIT_BYTES = 56 * 1024 * 1024

assert WINDOW == BLOCK and HEAD_DIM * 2 == LANES and KV_WIDTH == LANES


def _rms(xf, g):
    inv = lax.rsqrt(jnp.mean(xf * xf, axis=-1, keepdims=True) + EPS)
    return xf * inv * g


def _dot(a, b):
    return jnp.dot(a, b, preferred_element_type=jnp.float32)


def _layer_kernel(x_ref, g1_ref, w_in_ref, convw_ref, sinks_ref, ga_ref, gc_ref,
                  w_out_ref, g2_ref, w1_ref, w2_ref, gf_ref,
                  o_ref,
                  bias_scr, q_scr, k_scr, vt_scr, cu_scr, mixed_scr):
    t = pl.program_id(1)
    bf16 = jnp.bfloat16
    f32 = jnp.float32

    slot = lax.broadcasted_iota(jnp.int32, (BLOCK, BLOCK), 0)
    qpos = lax.broadcasted_iota(jnp.int32, (BLOCK, BLOCK), 1)
    is_cur = slot <= qpos

    @pl.when(t == 0)
    def _start_of_sequence():
        k_scr[:, 0:BLOCK, :] = jnp.zeros((KV_HEADS, BLOCK, LANES), bf16)
        vt_scr[:, 0:BLOCK] = jnp.zeros((KV_WIDTH, BLOCK), bf16)
        cu_scr[0:SUBLANES, :] = jnp.zeros((SUBLANES, CONV_WIDTH), f32)
        dist = jnp.where(is_cur, qpos - slot, qpos - slot + BLOCK).astype(f32)
        for h in range(ATTN_HEADS):
            slope = 2.0 ** (-8.0 * (h + 1) / ATTN_HEADS)
            b = -slope * dist
            bias_scr[0, h] = b
            bias_scr[1, h] = jnp.where(is_cur, b, MASKED)

    x = x_ref[...]
    y = _rms(x, g1_ref[...]).astype(bf16)

    low = lax.broadcasted_iota(jnp.int32, (TILE, LANES), 1) < HEAD_DIM
    zero = jnp.zeros((TILE, LANES), f32)

    q = _dot(y, w_in_ref[:, OFF_Q:OFF_Q + ATTN_WIDTH]) * (HEAD_DIM ** -0.5)
    for pair in range(ATTN_HEADS // 2):
        qp = q[:, pair * LANES:(pair + 1) * LANES]
        q_lo = jnp.where(low, qp, zero).astype(bf16)
        q_hi = jnp.where(low, zero, qp).astype(bf16)
        for j in range(N_BLOCKS):
            q_scr[j, 2 * pair] = q_lo[j * BLOCK:(j + 1) * BLOCK]
            q_scr[j, 2 * pair + 1] = q_hi[j * BLOCK:(j + 1) * BLOCK]

    kv = _dot(y, w_in_ref[:, OFF_KV:OFF_KV + 2 * KV_WIDTH])
    k = kv[:, :KV_WIDTH]
    k_swapped = pltpu.roll(k, HEAD_DIM, axis=1)
    k_scr[0, BLOCK:, :] = jnp.where(low, k, k_swapped).astype(bf16)
    k_scr[1, BLOCK:, :] = jnp.where(low, k_swapped, k).astype(bf16)
    vt_scr[:, BLOCK:] = kv[:, KV_WIDTH:].T.astype(bf16)

    cu = (_dot(y, w_in_ref[:, OFF_C:OFF_C + CONV_WIDTH])
          * _dot(y, w_in_ref[:, OFF_U:OFF_U + CONV_WIDTH]))
    cu_scr[SUBLANES:, :] = cu
    cw = convw_ref[...]
    conv = (cw[0:1, :] * cu_scr[SUBLANES - 2:SUBLANES - 2 + TILE, :]
            + cw[1:2, :] * cu_scr[SUBLANES - 1:SUBLANES - 1 + TILE, :]
            + cw[2:3, :] * cu)
    conv = _dot(y, w_in_ref[:, OFF_B:OFF_B + CONV_WIDTH]) * conv
    mixed_scr[:, ATTN_WIDTH:] = _rms(conv, gc_ref[...]).astype(bf16)

    first = jnp.where(t == 0, 1, 0)
    attn_blocks = []
    for j in range(N_BLOCKS):
        table = first if j == 0 else 0
        r0 = j * BLOCK
        pairs_out = []
        for g in range(KV_HEADS):
            kb = k_scr[g, r0:r0 + 2 * BLOCK, :]
            qs = q_scr[j, GROUP * g:GROUP * (g + 1)].reshape(GROUP * BLOCK, LANES)
            st = lax.dot_general(kb, qs, (((1,), (1,)), ((), ())),
                                 preferred_element_type=f32)
            probs = []
            invs = []
            for hh in range(GROUP):
                h = GROUP * g + hh
                cols = slice(hh * BLOCK, (hh + 1) * BLOCK)
                logits = (jnp.where(is_cur, st[BLOCK:, cols], st[:BLOCK, cols])
                          + bias_scr[table, h])
                sink = sinks_ref[h]
                m = jnp.maximum(jnp.max(logits, axis=0, keepdims=True), sink)
                p = jnp.exp(logits - m)
                den = jnp.sum(p, axis=0, keepdims=True) + jnp.exp(sink - m)
                invs.append(1.0 / den)
                zeros = jnp.zeros_like(p)
                probs.append(jnp.concatenate(
                    [jnp.where(is_cur, zeros, p), jnp.where(is_cur, p, zeros)],
                    axis=0).astype(bf16))
            pt = jnp.concatenate(probs, axis=1)
            vt = vt_scr[g * HEAD_DIM:(g + 1) * HEAD_DIM, r0:r0 + 2 * BLOCK]
            ot = _dot(vt, pt) * jnp.concatenate(invs, axis=1)
            for hp in range(GROUP // 2):
                two = ot[:, 2 * hp * BLOCK:(2 * hp + 2) * BLOCK]
                pairs_out.append(jnp.concatenate(
                    [two[:, :BLOCK], two[:, BLOCK:]], axis=0).T)
        attn_blocks.append(jnp.concatenate(pairs_out, axis=1))
    attn = jnp.concatenate(attn_blocks, axis=0)

    mixed_scr[:, :ATTN_WIDTH] = _rms(attn, ga_ref[...]).astype(bf16)

    h1 = x + _dot(mixed_scr[...], w_out_ref[...])

    z = _rms(h1, g2_ref[...]).astype(bf16)
    h2 = h1
    for c in range(D_FF // FF_CHUNK):
        a = _dot(z, w1_ref[:, c * FF_CHUNK:(c + 1) * FF_CHUNK])
        a = jnp.square(jnp.maximum(a, 0.0)).astype(bf16)
        h2 = h2 + _dot(a, w2_ref[c * FF_CHUNK:(c + 1) * FF_CHUNK, :])

    o_ref[...] = _rms(h2, gf_ref[...])

    k_scr[:, 0:BLOCK, :] = k_scr[:, TILE:TILE + BLOCK, :]
    vt_scr[:, 0:BLOCK] = vt_scr[:, TILE:TILE + BLOCK]
    cu_scr[0:SUBLANES, :] = cu_scr[TILE:TILE + SUBLANES, :]


def _resident(shape):
    return pl.BlockSpec(shape, lambda b, t: (0,) * len(shape),
                        pipeline_mode=pl.Buffered(1))


@jax.jit
def kernel(x, norm1_g, w_in, conv_w, sinks, attn_norm_g, conv_norm_g, w_out, norm2_g,
           w_ff1, w_ff2, final_g):
    batch, seq, d = x.shape
    assert d == D_MODEL and seq % TILE == 0
    assert norm1_g.shape[0] == 1, "single layer"
    bf16 = jnp.bfloat16
    row = lambda v: v.reshape(1, -1).astype(jnp.float32)

    tile_spec = pl.BlockSpec((None, TILE, D_MODEL), lambda b, t: (b, t, 0))
    in_specs = [
        tile_spec,
        _resident((1, D_MODEL)),
        _resident((D_MODEL, IN_WIDTH)),
        _resident((CONV_K, CONV_WIDTH)),
        pl.BlockSpec(memory_space=pltpu.SMEM),
        _resident((1, ATTN_WIDTH)),
        _resident((1, CONV_WIDTH)),
        _resident((D_MODEL, D_MODEL)),
        _resident((1, D_MODEL)),
        _resident((D_MODEL, D_FF)),
        _resident((D_FF, D_MODEL)),
        _resident((1, D_MODEL)),
    ]
    scratch_shapes = [
        pltpu.VMEM((2, ATTN_HEADS, BLOCK, BLOCK), jnp.float32),
        pltpu.VMEM((N_BLOCKS, ATTN_HEADS, BLOCK, LANES), bf16),
        pltpu.VMEM((KV_HEADS, BLOCK + TILE, LANES), bf16),
        pltpu.VMEM((KV_WIDTH, BLOCK + TILE), bf16),
        pltpu.VMEM((SUBLANES + TILE, CONV_WIDTH), jnp.float32),
        pltpu.VMEM((TILE, D_MODEL), bf16),
    ]
    return pl.pallas_call(
        _layer_kernel,
        grid=(batch, seq // TILE),
        in_specs=in_specs,
        out_specs=tile_spec,
        out_shape=jax.ShapeDtypeStruct(x.shape, x.dtype),
        scratch_shapes=scratch_shapes,
        compiler_params=pltpu.CompilerParams(
            dimension_semantics=("arbitrary", "arbitrary"),
            vmem_limit_bytes=VMEM_LIMIT_BYTES),
        name="hybrid_layer",
    )(x, row(norm1_g[0]), w_in[0].astype(bf16), conv_w[0].astype(jnp.float32),
      sinks[0].astype(jnp.float32), row(attn_norm_g[0]), row(conv_norm_g[0]),
      w_out[0].astype(bf16), row(norm2_g[0]), w_ff1[0].astype(bf16),
      w_ff2[0].astype(bf16), row(final_g))
```

```python
import jax
import jax.numpy as jnp
from jax import lax
from jax.experimental import pallas as pl
from jax.experimental.pallas import tpu as pltpu

D_MODEL = 1024
ATTN_HEADS = 8
KV_HEADS = 2
HEAD_DIM = 64
ATTN_WIDTH = ATTN_HEADS * HEAD_DIM
KV_WIDTH = KV_HEADS * HEAD_DIM
GROUP = ATTN_HEADS // KV_HEADS
WINDOW = 128
BLOCK = 128
CONV_WIDTH = D_MODEL - ATTN_WIDTH
CONV_K = 3
D_FF = 4 * D_MODEL
EPS = 1e-5
OFF_Q = 0
OFF_KV = ATTN_WIDTH
OFF_C = ATTN_WIDTH + 2 * KV_WIDTH
OFF_B = OFF_C + CONV_WIDTH
OFF_U = OFF_B + CONV_WIDTH
IN_WIDTH = OFF_U + CONV_WIDTH

LANES = 128
SUBLANES = 8
TILE = 512
N_BLOCKS = TILE // BLOCK
FF_CHUNK = 1024
GATE_TILE = 256
MASKED = -1e30
VMEM_LIMIT_BYTES = 56 * 1024 * 1024

assert WINDOW == BLOCK and HEAD_DIM * 2 == LANES and KV_WIDTH == LANES


def _rms(xf, g):
    inv = lax.rsqrt(jnp.mean(xf * xf, axis=-1, keepdims=True) + EPS)
    return xf * inv * g


def _dot(a, b):
    return jnp.dot(a, b, preferred_element_type=jnp.float32)


def _layer_kernel(x_ref, g1_ref, w_in_ref, convw_ref, sinks_ref, ga_ref, gc_ref,
                  w_out_ref, g2_ref, w1_ref, w2_ref, gf_ref,
                  o_ref,
                  bias_scr, q_scr, k_scr, vt_scr, attn_scr, gates_scr, cu_scr, mixed_scr):
    t = pl.program_id(1)
    bf16 = jnp.bfloat16
    f32 = jnp.float32

    slot = lax.broadcasted_iota(jnp.int32, (BLOCK, BLOCK), 0)
    qpos = lax.broadcasted_iota(jnp.int32, (BLOCK, BLOCK), 1)
    is_cur = slot <= qpos

    @pl.when(t == 0)
    def _start_of_sequence():
        k_scr[:, 0:BLOCK, :] = jnp.zeros((KV_HEADS, BLOCK, LANES), bf16)
        vt_scr[:, 0:BLOCK] = jnp.zeros((KV_WIDTH, BLOCK), bf16)
        cu_scr[0:SUBLANES, :] = jnp.zeros((SUBLANES, CONV_WIDTH), f32)
        dist = jnp.where(is_cur, qpos - slot, qpos - slot + BLOCK).astype(f32)
        for h in range(ATTN_HEADS):
            slope = 2.0 ** (-8.0 * (h + 1) / ATTN_HEADS)
            b = -slope * dist
            bias_scr[0, h] = b
            bias_scr[1, h] = jnp.where(is_cur, b, MASKED)

    x = x_ref[...]
    y = _rms(x, g1_ref[...]).astype(bf16)

    low = lax.broadcasted_iota(jnp.int32, (TILE, LANES), 1) < HEAD_DIM
    zero = jnp.zeros((TILE, LANES), f32)

    q = _dot(y, w_in_ref[:, OFF_Q:OFF_Q + ATTN_WIDTH]) * (HEAD_DIM ** -0.5)
    for pair in range(ATTN_HEADS // 2):
        qp = q[:, pair * LANES:(pair + 1) * LANES]
        q_lo = jnp.where(low, qp, zero).astype(bf16)
        q_hi = jnp.where(low, zero, qp).astype(bf16)
        for j in range(N_BLOCKS):
            q_scr[j, 2 * pair] = q_lo[j * BLOCK:(j + 1) * BLOCK]
            q_scr[j, 2 * pair + 1] = q_hi[j * BLOCK:(j + 1) * BLOCK]

    kv = _dot(y, w_in_ref[:, OFF_KV:OFF_KV + 2 * KV_WIDTH])
    k = kv[:, :KV_WIDTH]
    k_swapped = pltpu.roll(k, HEAD_DIM, axis=1)
    k_scr[0, BLOCK:, :] = jnp.where(low, k, k_swapped).astype(bf16)
    k_scr[1, BLOCK:, :] = jnp.where(low, k_swapped, k).astype(bf16)
    vt_scr[:, BLOCK:] = kv[:, KV_WIDTH:].T.astype(bf16)

    first = jnp.where(t == 0, 1, 0)
    units = [(j, g) for j in range(N_BLOCKS) for g in range(KV_HEADS)]
    gate_tiles = list(range(OFF_C, IN_WIDTH, GATE_TILE))

    def gate_piece(i):
        if i < len(gate_tiles):
            c0 = gate_tiles[i]
            gates_scr[:, c0 - OFF_C:c0 - OFF_C + GATE_TILE] = _dot(
                y, w_in_ref[:, c0:c0 + GATE_TILE])

    def scores(j, g):
        kb = k_scr[g, j * BLOCK:(j + 2) * BLOCK, :]
        qs = q_scr[j, GROUP * g:GROUP * (g + 1)].reshape(GROUP * BLOCK, LANES)
        return lax.dot_general(kb, qs, (((1,), (1,)), ((), ())),
                               preferred_element_type=f32)

    def softmax(j, g, st):
        table = first if j == 0 else 0
        probs = []
        invs = []
        for hh in range(GROUP):
            h = GROUP * g + hh
            cols = slice(hh * BLOCK, (hh + 1) * BLOCK)
            logits = (jnp.where(is_cur, st[BLOCK:, cols], st[:BLOCK, cols])
                      + bias_scr[table, h])
            sink = sinks_ref[h]
            m = jnp.maximum(jnp.max(logits, axis=0, keepdims=True), sink)
            p = jnp.exp(logits - m)
            den = jnp.sum(p, axis=0, keepdims=True) + jnp.exp(sink - m)
            invs.append(1.0 / den)
            zeros = jnp.zeros_like(p)
            probs.append(jnp.concatenate(
                [jnp.where(is_cur, zeros, p), jnp.where(is_cur, p, zeros)],
                axis=0).astype(bf16))
        return jnp.concatenate(probs, axis=1), jnp.concatenate(invs, axis=1)

    def values(j, g, pt, inv):
        vt = vt_scr[g * HEAD_DIM:(g + 1) * HEAD_DIM, j * BLOCK:(j + 2) * BLOCK]
        ot = _dot(vt, pt) * inv
        for hp in range(GROUP // 2):
            two = ot[:, 2 * hp * BLOCK:(2 * hp + 2) * BLOCK]
            pair = (GROUP // 2) * g + hp
            attn_scr[j * BLOCK:(j + 1) * BLOCK, pair * LANES:(pair + 1) * LANES] = (
                jnp.concatenate([two[:, :BLOCK], two[:, BLOCK:]], axis=0).T)

    gate_piece(0)
    st = {0: scores(*units[0]), 1: scores(*units[1])}
    sm = {}
    for i, (j, g) in enumerate(units):
        sm[i] = softmax(j, g, st.pop(i))
        if i + 2 < len(units):
            st[i + 2] = scores(*units[i + 2])
        gate_piece(i + 1)
        if i >= 1:
            values(*units[i - 1], *sm.pop(i - 1))
    values(*units[-1], *sm.pop(len(units) - 1))
    assert len(gate_tiles) <= len(units) + 1
    attn = attn_scr[...]

    mixed_scr[:, :ATTN_WIDTH] = _rms(attn, ga_ref[...]).astype(bf16)

    cu = (gates_scr[:, OFF_C - OFF_C:OFF_B - OFF_C]
          * gates_scr[:, OFF_U - OFF_C:IN_WIDTH - OFF_C])
    cu_scr[SUBLANES:, :] = cu
    cw = convw_ref[...]
    conv = (cw[0:1, :] * cu_scr[SUBLANES - 2:SUBLANES - 2 + TILE, :]
            + cw[1:2, :] * cu_scr[SUBLANES - 1:SUBLANES - 1 + TILE, :]
            + cw[2:3, :] * cu)
    conv = gates_scr[:, OFF_B - OFF_C:OFF_U - OFF_C] * conv
    mixed_scr[:, ATTN_WIDTH:] = _rms(conv, gc_ref[...]).astype(bf16)

    h1 = x + _dot(mixed_scr[...], w_out_ref[...])

    z = _rms(h1, g2_ref[...]).astype(bf16)
    h2 = h1
    for c in range(D_FF // FF_CHUNK):
        a = _dot(z, w1_ref[:, c * FF_CHUNK:(c + 1) * FF_CHUNK])
        a = jnp.square(jnp.maximum(a, 0.0)).astype(bf16)
        h2 = h2 + _dot(a, w2_ref[c * FF_CHUNK:(c + 1) * FF_CHUNK, :])

    o_ref[...] = _rms(h2, gf_ref[...])

    k_scr[:, 0:BLOCK, :] = k_scr[:, TILE:TILE + BLOCK, :]
    vt_scr[:, 0:BLOCK] = vt_scr[:, TILE:TILE + BLOCK]
    cu_scr[0:SUBLANES, :] = cu_scr[TILE:TILE + SUBLANES, :]


def _resident(shape):
    return pl.BlockSpec(shape, lambda b, t: (0,) * len(shape),
                        pipeline_mode=pl.Buffered(1))


@jax.jit
def kernel(x, norm1_g, w_in, conv_w, sinks, attn_norm_g, conv_norm_g, w_out, norm2_g,
           w_ff1, w_ff2, final_g):
    batch, seq, d = x.shape
    assert d == D_MODEL and seq % TILE == 0
    assert norm1_g.shape[0] == 1, "single layer"
    bf16 = jnp.bfloat16
    row = lambda v: v.reshape(1, -1).astype(jnp.float32)

    tile_spec = pl.BlockSpec((None, TILE, D_MODEL), lambda b, t: (b, t, 0))
    in_specs = [
        tile_spec,
        _resident((1, D_MODEL)),
        _resident((D_MODEL, IN_WIDTH)),
        _resident((CONV_K, CONV_WIDTH)),
        pl.BlockSpec(memory_space=pltpu.SMEM),
        _resident((1, ATTN_WIDTH)),
        _resident((1, CONV_WIDTH)),
        _resident((D_MODEL, D_MODEL)),
        _resident((1, D_MODEL)),
        _resident((D_MODEL, D_FF)),
        _resident((D_FF, D_MODEL)),
        _resident((1, D_MODEL)),
    ]
    scratch_shapes = [
        pltpu.VMEM((2, ATTN_HEADS, BLOCK, BLOCK), jnp.float32),
        pltpu.VMEM((N_BLOCKS, ATTN_HEADS, BLOCK, LANES), bf16),
        pltpu.VMEM((KV_HEADS, BLOCK + TILE, LANES), bf16),
        pltpu.VMEM((KV_WIDTH, BLOCK + TILE), bf16),
        pltpu.VMEM((TILE, ATTN_WIDTH), jnp.float32),
        pltpu.VMEM((TILE, 3 * CONV_WIDTH), jnp.float32),
        pltpu.VMEM((SUBLANES + TILE, CONV_WIDTH), jnp.float32),
        pltpu.VMEM((TILE, D_MODEL), bf16),
    ]
    return pl.pallas_call(
        _layer_kernel,
        grid=(batch, seq // TILE),
        in_specs=in_specs,
        out_specs=tile_spec,
        out_shape=jax.ShapeDtypeStruct(x.shape, x.dtype),
        scratch_shapes=scratch_shapes,
        compiler_params=pltpu.CompilerParams(
            dimension_semantics=("arbitrary", "arbitrary"),
            vmem_limit_bytes=VMEM_LIMIT_BYTES),
        name="hybrid_layer",
    )(x, row(norm1_g[0]), w_in[0].astype(bf16), conv_w[0].astype(jnp.float32),
      sinks[0].astype(jnp.float32), row(attn_norm_g[0]), row(conv_norm_g[0]),
      w_out[0].astype(bf16), row(norm2_g[0]), w_ff1[0].astype(bf16),
      w_ff2[0].astype(bf16), row(final_g))
```

```python
import jax
import jax.numpy as jnp
from jax import lax
from jax.experimental import pallas as pl
from jax.experimental.pallas import tpu as pltpu

D_MODEL = 1024
ATTN_HEADS = 8
KV_HEADS = 2
HEAD_DIM = 64
ATTN_WIDTH = ATTN_HEADS * HEAD_DIM
KV_WIDTH = KV_HEADS * HEAD_DIM
GROUP = ATTN_HEADS // KV_HEADS
WINDOW = 128
BLOCK = 128
CONV_WIDTH = D_MODEL - ATTN_WIDTH
CONV_K = 3
D_FF = 4 * D_MODEL
EPS = 1e-5
OFF_Q = 0
OFF_KV = ATTN_WIDTH
OFF_C = ATTN_WIDTH + 2 * KV_WIDTH
OFF_B = OFF_C + CONV_WIDTH
OFF_U = OFF_B + CONV_WIDTH
IN_WIDTH = OFF_U + CONV_WIDTH

LANES = 128
SUBLANES = 8
TILE = 512
N_CHAINS = 2
CHAIN = TILE // N_CHAINS
CHAIN_BLOCKS = CHAIN // BLOCK
FF_CHUNK = 1024
GATE_TILE = 256
MASKED = -1e30
VMEM_LIMIT_BYTES = 56 * 1024 * 1024

assert WINDOW == BLOCK and HEAD_DIM * 2 == LANES and KV_WIDTH == LANES

CONV_PIECE = 256
_STAGES_PER_CHAIN = 19
_EMISSION = "AABABAAABABABABAABABABABABABABABABNBBBB"
assert _EMISSION.count("A") == _EMISSION.count("B") == _STAGES_PER_CHAIN
assert _EMISSION.count("N") == 1


def _rms(xf, g):
    inv = lax.rsqrt(jnp.mean(xf * xf, axis=-1, keepdims=True) + EPS)
    return xf * inv * g


def _dot(a, b):
    return jnp.dot(a, b, preferred_element_type=jnp.float32)


def _layer_kernel(x_ref, xn_ref, g1_ref, w_in_ref, convw_ref, sinks_ref, ga_ref, gc_ref,
                  w_out_ref, g2_ref, w1_ref, w2_ref, gf_ref,
                  o_ref,
                  bias_scr, q_scr, k_scr, vt_scr, attn_scr, gates_scr, cu_scr, mixed_scr,
                  ynext_scr):
    t = pl.program_id(1)
    bf16 = jnp.bfloat16
    f32 = jnp.float32

    slot = lax.broadcasted_iota(jnp.int32, (BLOCK, BLOCK), 0)
    qpos = lax.broadcasted_iota(jnp.int32, (BLOCK, BLOCK), 1)
    is_cur = slot <= qpos

    @pl.when(t == 0)
    def _start_of_sequence():
        k_scr[:, 0:BLOCK, :] = jnp.zeros((KV_HEADS, BLOCK, LANES), bf16)
        vt_scr[:, 0:BLOCK] = jnp.zeros((KV_WIDTH, BLOCK), bf16)
        cu_scr[0:SUBLANES, :] = jnp.zeros((SUBLANES, CONV_WIDTH), f32)
        dist = jnp.where(is_cur, qpos - slot, qpos - slot + BLOCK).astype(f32)
        for h in range(ATTN_HEADS):
            slope = 2.0 ** (-8.0 * (h + 1) / ATTN_HEADS)
            b = -slope * dist
            bias_scr[0, h] = b
            bias_scr[1, h] = jnp.where(is_cur, b, MASKED)

    @pl.when((pl.program_id(0) == 0) & (t == 0))
    def _first_step():
        ynext_scr[...] = _rms(x_ref[0:CHAIN, :], g1_ref[...]).astype(bf16)

    first = jnp.where(t == 0, 1, 0)
    low = lax.broadcasted_iota(jnp.int32, (CHAIN, LANES), 1) < HEAD_DIM
    zero = jnp.zeros((CHAIN, LANES), f32)
    gate_tiles = list(range(OFF_C, IN_WIDTH, GATE_TILE))

    def chain(c):
        r0 = c * CHAIN
        rows = slice(r0, r0 + CHAIN)
        blocks = range(c * CHAIN_BLOCKS, (c + 1) * CHAIN_BLOCKS)

        x = x_ref[rows, :]
        if c == 0:
            y = ynext_scr[...]
        else:
            y = _rms(x, g1_ref[...]).astype(bf16)
        yield

        q = _dot(y, w_in_ref[:, OFF_Q:OFF_Q + ATTN_WIDTH]) * (HEAD_DIM ** -0.5)
        for pair in range(ATTN_HEADS // 2):
            qp = q[:, pair * LANES:(pair + 1) * LANES]
            q_lo = jnp.where(low, qp, zero).astype(bf16)
            q_hi = jnp.where(low, zero, qp).astype(bf16)
            for jj, j in enumerate(blocks):
                q_scr[j, 2 * pair] = q_lo[jj * BLOCK:(jj + 1) * BLOCK]
                q_scr[j, 2 * pair + 1] = q_hi[jj * BLOCK:(jj + 1) * BLOCK]
        kv = _dot(y, w_in_ref[:, OFF_KV:OFF_KV + 2 * KV_WIDTH])
        k = kv[:, :KV_WIDTH]
        k_swapped = pltpu.roll(k, HEAD_DIM, axis=1)
        k_scr[0, BLOCK + r0:BLOCK + r0 + CHAIN, :] = jnp.where(low, k, k_swapped).astype(bf16)
        k_scr[1, BLOCK + r0:BLOCK + r0 + CHAIN, :] = jnp.where(low, k_swapped, k).astype(bf16)
        vt_scr[:, BLOCK + r0:BLOCK + r0 + CHAIN] = kv[:, KV_WIDTH:].T.astype(bf16)
        yield

        def gate_piece(i):
            if i < len(gate_tiles):
                c0 = gate_tiles[i]
                gates_scr[rows, c0 - OFF_C:c0 - OFF_C + GATE_TILE] = _dot(
                    y, w_in_ref[:, c0:c0 + GATE_TILE])

        def scores(j, g):
            kb = k_scr[g, j * BLOCK:(j + 2) * BLOCK, :]
            qs = q_scr[j, GROUP * g:GROUP * (g + 1)].reshape(GROUP * BLOCK, LANES)
            return lax.dot_general(kb, qs, (((1,), (1,)), ((), ())),
                                   preferred_element_type=f32)

        def softmax(j, g, st):
            table = first if j == 0 else 0
            probs = []
            invs = []
            for hh in range(GROUP):
                h = GROUP * g + hh
                cols = slice(hh * BLOCK, (hh + 1) * BLOCK)
                logits = (jnp.where(is_cur, st[BLOCK:, cols], st[:BLOCK, cols])
                          + bias_scr[table, h])
                sink = sinks_ref[h]
                m = jnp.maximum(jnp.max(logits, axis=0, keepdims=True), sink)
                p = jnp.exp(logits - m)
                den = jnp.sum(p, axis=0, keepdims=True) + jnp.exp(sink - m)
                invs.append(1.0 / den)
                zeros = jnp.zeros_like(p)
                probs.append(jnp.concatenate(
                    [jnp.where(is_cur, zeros, p), jnp.where(is_cur, p, zeros)],
                    axis=0).astype(bf16))
            return jnp.concatenate(probs, axis=1), jnp.concatenate(invs, axis=1)

        def values(j, g, pt, inv):
            vt = vt_scr[g * HEAD_DIM:(g + 1) * HEAD_DIM, j * BLOCK:(j + 2) * BLOCK]
            ot = _dot(vt, pt) * inv
            for hp in range(GROUP // 2):
                two = ot[:, 2 * hp * BLOCK:(2 * hp + 2) * BLOCK]
                pair = (GROUP // 2) * g + hp
                attn_scr[j * BLOCK:(j + 1) * BLOCK, pair * LANES:(pair + 1) * LANES] = (
                    jnp.concatenate([two[:, :BLOCK], two[:, BLOCK:]], axis=0).T)

        units = [(j, g) for j in blocks for g in range(KV_HEADS)]
        assert len(gate_tiles) <= len(units) + 2
        gate_piece(0)
        st = {0: scores(*units[0]), 1: scores(*units[1])}
        sm = {}
        for i, (j, g) in enumerate(units):
            sm[i] = softmax(j, g, st.pop(i))
            if i + 2 < len(units):
                st[i + 2] = scores(*units[i + 2])
            gate_piece(i + 1)
            if i >= 1:
                values(*units[i - 1], *sm.pop(i - 1))
            if i + 1 == len(units):
                gate_piece(i + 2)
                values(*units[i], *sm.pop(i))
            yield

        conv_pieces = []
        for p0 in range(0, CONV_WIDTH, CONV_PIECE):
            cols = slice(p0, p0 + CONV_PIECE)
            cu = (gates_scr[rows, OFF_C - OFF_C + p0:OFF_C - OFF_C + p0 + CONV_PIECE]
                  * gates_scr[rows, OFF_U - OFF_C + p0:OFF_U - OFF_C + p0 + CONV_PIECE])
            cu_scr[SUBLANES + r0:SUBLANES + r0 + CHAIN, cols] = cu
            cw = convw_ref[:, cols]
            ext = jnp.concatenate([cu_scr[r0:r0 + SUBLANES, cols], cu], axis=0)
            prev1 = pltpu.roll(ext, 1, axis=0)[SUBLANES:]
            prev2 = pltpu.roll(ext, 2, axis=0)[SUBLANES:]
            conv = cw[0:1, :] * prev2 + cw[1:2, :] * prev1 + cw[2:3, :] * cu
            conv_pieces.append(
                gates_scr[rows, OFF_B - OFF_C + p0:OFF_B - OFF_C + p0 + CONV_PIECE] * conv)
            if p0 + CONV_PIECE < CONV_WIDTH:
                yield
        conv = jnp.concatenate(conv_pieces, axis=1)
        mixed_scr[rows, ATTN_WIDTH:] = _rms(conv, gc_ref[...]).astype(bf16)
        mixed_scr[rows, :ATTN_WIDTH] = _rms(attn_scr[rows, :], ga_ref[...]).astype(bf16)
        yield

        h1 = x + _dot(mixed_scr[rows, :], w_out_ref[...])
        yield

        z = _rms(h1, g2_ref[...]).astype(bf16)
        yield
        h2 = h1
        for cc in range(D_FF // FF_CHUNK):
            a = _dot(z, w1_ref[:, cc * FF_CHUNK:(cc + 1) * FF_CHUNK])
            a = jnp.square(jnp.maximum(a, 0.0)).astype(bf16)
            yield
            h2 = h2 + _dot(a, w2_ref[cc * FF_CHUNK:(cc + 1) * FF_CHUNK, :])
            yield

        o_ref[rows, :] = _rms(h2, gf_ref[...])

    def next_tile_prenorm():
        ynext_scr[...] = _rms(xn_ref[...], g1_ref[...]).astype(bf16)

    chains = {"A": chain(0), "B": chain(1)}
    finished = set()
    for name in _EMISSION:
        if name == "N":
            next_tile_prenorm()
            continue
        assert name not in finished
        if next(chains[name], "done") == "done":
            finished.add(name)
    assert finished == set(chains), "emission plan does not cover every stage"

    k_scr[:, 0:BLOCK, :] = k_scr[:, TILE:TILE + BLOCK, :]
    vt_scr[:, 0:BLOCK] = vt_scr[:, TILE:TILE + BLOCK]
    cu_scr[0:SUBLANES, :] = cu_scr[TILE:TILE + SUBLANES, :]


def _resident(shape):
    return pl.BlockSpec(shape, lambda b, t: (0,) * len(shape),
                        pipeline_mode=pl.Buffered(1))


@jax.jit
def kernel(x, norm1_g, w_in, conv_w, sinks, attn_norm_g, conv_norm_g, w_out, norm2_g,
           w_ff1, w_ff2, final_g):
    batch, seq, d = x.shape
    assert d == D_MODEL and seq % TILE == 0
    assert norm1_g.shape[0] == 1, "single layer"
    bf16 = jnp.bfloat16
    row = lambda v: v.reshape(1, -1).astype(jnp.float32)

    tile_spec = pl.BlockSpec((None, TILE, D_MODEL), lambda b, t: (b, t, 0))
    n_tiles = seq // TILE

    def next_first_chain(b, t):
        nxt = jnp.minimum(b * n_tiles + t + 1, batch * n_tiles - 1)
        return (nxt // n_tiles, (nxt % n_tiles) * N_CHAINS, 0)

    in_specs = [
        tile_spec,
        pl.BlockSpec((None, CHAIN, D_MODEL), next_first_chain),
        _resident((1, D_MODEL)),
        _resident((D_MODEL, IN_WIDTH)),
        _resident((CONV_K, CONV_WIDTH)),
        pl.BlockSpec(memory_space=pltpu.SMEM),
        _resident((1, ATTN_WIDTH)),
        _resident((1, CONV_WIDTH)),
        _resident((D_MODEL, D_MODEL)),
        _resident((1, D_MODEL)),
        _resident((D_MODEL, D_FF)),
        _resident((D_FF, D_MODEL)),
        _resident((1, D_MODEL)),
    ]
    scratch_shapes = [
        pltpu.VMEM((2, ATTN_HEADS, BLOCK, BLOCK), jnp.float32),
        pltpu.VMEM((TILE // BLOCK, ATTN_HEADS, BLOCK, LANES), bf16),
        pltpu.VMEM((KV_HEADS, BLOCK + TILE, LANES), bf16),
        pltpu.VMEM((KV_WIDTH, BLOCK + TILE), bf16),
        pltpu.VMEM((TILE, ATTN_WIDTH), jnp.float32),
        pltpu.VMEM((TILE, 3 * CONV_WIDTH), jnp.float32),
        pltpu.VMEM((SUBLANES + TILE, CONV_WIDTH), jnp.float32),
        pltpu.VMEM((TILE, D_MODEL), bf16),
        pltpu.VMEM((CHAIN, D_MODEL), bf16),
    ]
    return pl.pallas_call(
        _layer_kernel,
        grid=(batch, seq // TILE),
        in_specs=in_specs,
        out_specs=tile_spec,
        out_shape=jax.ShapeDtypeStruct(x.shape, x.dtype),
        scratch_shapes=scratch_shapes,
        compiler_params=pltpu.CompilerParams(
            dimension_semantics=("arbitrary", "arbitrary"),
            vmem_limit_bytes=VMEM_LIMIT_BYTES),
        name="hybrid_layer",
    )(x, x, row(norm1_g[0]), w_in[0].astype(bf16), conv_w[0].astype(jnp.float32),
      sinks[0].astype(jnp.float32), row(attn_norm_g[0]), row(conv_norm_g[0]),
      w_out[0].astype(bf16), row(norm2_g[0]), w_ff1[0].astype(bf16),
      w_ff2[0].astype(bf16), row(final_g))
```

```python
import jax
import jax.numpy as jnp
from jax import lax
from jax.experimental import pallas as pl
from jax.experimental.pallas import tpu as pltpu

D_MODEL = 1024
ATTN_HEADS = 8
KV_HEADS = 2
HEAD_DIM = 64
ATTN_WIDTH = ATTN_HEADS * HEAD_DIM
KV_WIDTH = KV_HEADS * HEAD_DIM
GROUP = ATTN_HEADS // KV_HEADS
WINDOW = 128
BLOCK = 128
CONV_WIDTH = D_MODEL - ATTN_WIDTH
CONV_K = 3
D_FF = 4 * D_MODEL
EPS = 1e-5
OFF_Q = 0
OFF_KV = ATTN_WIDTH
OFF_C = ATTN_WIDTH + 2 * KV_WIDTH
OFF_B = OFF_C + CONV_WIDTH
OFF_U = OFF_B + CONV_WIDTH
IN_WIDTH = OFF_U + CONV_WIDTH

LANES = 128
SUBLANES = 8
TILE = 512
N_CHAINS = 2
CHAIN = TILE // N_CHAINS
CHAIN_BLOCKS = CHAIN // BLOCK
FF_CHUNK = 1024
GATE_TILE = 256
MASKED = -1e30
VMEM_LIMIT_BYTES = 56 * 1024 * 1024

assert WINDOW == BLOCK and HEAD_DIM * 2 == LANES and KV_WIDTH == LANES

CONV_PIECE = 256
_STAGES_PER_CHAIN = 19
_EMISSION = "AABBAAAABABABABAABABABABABABABBAABBBBB"
assert _EMISSION.count("A") == _EMISSION.count("B") == _STAGES_PER_CHAIN


def _stage_positions(name):
    return [i for i, ch in enumerate(_EMISSION) if ch == name]


for _a, _b in ((1, 2), (6, 6), (7, 7)):
    assert _stage_positions("A")[_a] < _stage_positions("B")[_b]


def _rms(xf, g):
    inv = lax.rsqrt(jnp.mean(xf * xf, axis=-1, keepdims=True) + EPS)
    return xf * inv * g


def _dot(a, b):
    return jnp.dot(a, b, preferred_element_type=jnp.float32)


def _layer_kernel(x_ref, g1_ref, w_in_ref, convw_ref, sinks_ref, ga_ref, gc_ref,
                  w_out_ref, g2_ref, w1_ref, w2_ref, gf_ref,
                  o_ref,
                  bias_scr, q_scr, k_scr, vt_scr, attn_scr, gates_scr, cu_scr, mixed_scr,
                  y_scr, z_scr, a_scr):
    t = pl.program_id(1)
    bf16 = jnp.bfloat16
    f32 = jnp.float32

    slot = lax.broadcasted_iota(jnp.int32, (BLOCK, BLOCK), 0)
    qpos = lax.broadcasted_iota(jnp.int32, (BLOCK, BLOCK), 1)
    is_cur = slot <= qpos

    @pl.when(t == 0)
    def _start_of_sequence():
        k_scr[:, 0:BLOCK, :] = jnp.zeros((KV_HEADS, BLOCK, LANES), bf16)
        vt_scr[:, 0:BLOCK] = jnp.zeros((KV_WIDTH, BLOCK), bf16)
        cu_scr[0:SUBLANES, :] = jnp.zeros((SUBLANES, CONV_WIDTH), f32)
        dist = jnp.where(is_cur, qpos - slot, qpos - slot + BLOCK).astype(f32)
        for h in range(ATTN_HEADS):
            slope = 2.0 ** (-8.0 * (h + 1) / ATTN_HEADS)
            b = -slope * dist
            bias_scr[0, h] = b
            bias_scr[1, h] = jnp.where(is_cur, b, MASKED)

    first = jnp.where(t == 0, 1, 0)
    low = lax.broadcasted_iota(jnp.int32, (CHAIN, LANES), 1) < HEAD_DIM
    zero = jnp.zeros((CHAIN, LANES), f32)
    gate_tiles = list(range(OFF_C, IN_WIDTH, GATE_TILE))

    def chain(c):
        r0 = c * CHAIN
        rows = slice(r0, r0 + CHAIN)
        blocks = range(c * CHAIN_BLOCKS, (c + 1) * CHAIN_BLOCKS)

        x = x_ref[rows, :]
        y_scr[rows, :] = _rms(x, g1_ref[...]).astype(bf16)
        yield

        q = _dot(y_scr[rows, :], w_in_ref[:, OFF_Q:OFF_Q + ATTN_WIDTH]) * (HEAD_DIM ** -0.5)
        for pair in range(ATTN_HEADS // 2):
            qp = q[:, pair * LANES:(pair + 1) * LANES]
            q_lo = jnp.where(low, qp, zero).astype(bf16)
            q_hi = jnp.where(low, zero, qp).astype(bf16)
            for jj, j in enumerate(blocks):
                q_scr[j, 2 * pair] = q_lo[jj * BLOCK:(jj + 1) * BLOCK]
                q_scr[j, 2 * pair + 1] = q_hi[jj * BLOCK:(jj + 1) * BLOCK]
        kv = _dot(y_scr[rows, :], w_in_ref[:, OFF_KV:OFF_KV + 2 * KV_WIDTH])
        k = kv[:, :KV_WIDTH]
        k_swapped = pltpu.roll(k, HEAD_DIM, axis=1)
        k_scr[0, BLOCK + r0:BLOCK + r0 + CHAIN, :] = jnp.where(low, k, k_swapped).astype(bf16)
        k_scr[1, BLOCK + r0:BLOCK + r0 + CHAIN, :] = jnp.where(low, k_swapped, k).astype(bf16)
        vt_scr[:, BLOCK + r0:BLOCK + r0 + CHAIN] = kv[:, KV_WIDTH:].T.astype(bf16)
        yield

        def gate_piece(i):
            if i < len(gate_tiles):
                c0 = gate_tiles[i]
                gates_scr[rows, c0 - OFF_C:c0 - OFF_C + GATE_TILE] = _dot(
                    y_scr[rows, :], w_in_ref[:, c0:c0 + GATE_TILE])

        def scores(j, g):
            kb = k_scr[g, j * BLOCK:(j + 2) * BLOCK, :]
            qs = q_scr[j, GROUP * g:GROUP * (g + 1)].reshape(GROUP * BLOCK, LANES)
            return lax.dot_general(kb, qs, (((1,), (1,)), ((), ())),
                                   preferred_element_type=f32)

        def softmax(j, g, st):
            table = first if j == 0 else 0
            probs = []
            invs = []
            for hh in range(GROUP):
                h = GROUP * g + hh
                cols = slice(hh * BLOCK, (hh + 1) * BLOCK)
                logits = (jnp.where(is_cur, st[BLOCK:, cols], st[:BLOCK, cols])
                          + bias_scr[table, h])
                sink = sinks_ref[h]
                m = jnp.maximum(jnp.max(logits, axis=0, keepdims=True), sink)
                p = jnp.exp(logits - m)
                den = jnp.sum(p, axis=0, keepdims=True) + jnp.exp(sink - m)
                invs.append(1.0 / den)
                zeros = jnp.zeros_like(p)
                probs.append(jnp.concatenate(
                    [jnp.where(is_cur, zeros, p), jnp.where(is_cur, p, zeros)],
                    axis=0).astype(bf16))
            return jnp.concatenate(probs, axis=1), jnp.concatenate(invs, axis=1)

        def values(j, g, pt, inv):
            vt = vt_scr[g * HEAD_DIM:(g + 1) * HEAD_DIM, j * BLOCK:(j + 2) * BLOCK]
            ot = _dot(vt, pt) * inv
            for hp in range(GROUP // 2):
                two = ot[:, 2 * hp * BLOCK:(2 * hp + 2) * BLOCK]
                pair = (GROUP // 2) * g + hp
                attn_scr[j * BLOCK:(j + 1) * BLOCK, pair * LANES:(pair + 1) * LANES] = (
                    jnp.concatenate([two[:, :BLOCK], two[:, BLOCK:]], axis=0).T)

        units = [(j, g) for j in blocks for g in range(KV_HEADS)]
        assert len(gate_tiles) <= len(units) + 2
        gate_piece(0)
        st = {0: scores(*units[0]), 1: scores(*units[1])}
        sm = {}
        for i, (j, g) in enumerate(units):
            sm[i] = softmax(j, g, st.pop(i))
            if i + 2 < len(units):
                st[i + 2] = scores(*units[i + 2])
            gate_piece(i + 1)
            if i >= 1:
                values(*units[i - 1], *sm.pop(i - 1))
            if i + 1 == len(units):
                gate_piece(i + 2)
                values(*units[i], *sm.pop(i))
            yield

        conv_pieces = []
        for p0 in range(0, CONV_WIDTH, CONV_PIECE):
            cols = slice(p0, p0 + CONV_PIECE)
            cu = (gates_scr[rows, OFF_C - OFF_C + p0:OFF_C - OFF_C + p0 + CONV_PIECE]
                  * gates_scr[rows, OFF_U - OFF_C + p0:OFF_U - OFF_C + p0 + CONV_PIECE])
            cu_scr[SUBLANES + r0:SUBLANES + r0 + CHAIN, cols] = cu
            cw = convw_ref[:, cols]
            ext = jnp.concatenate([cu_scr[r0:r0 + SUBLANES, cols], cu], axis=0)
            prev1 = pltpu.roll(ext, 1, axis=0)[SUBLANES:]
            prev2 = pltpu.roll(ext, 2, axis=0)[SUBLANES:]
            conv = cw[0:1, :] * prev2 + cw[1:2, :] * prev1 + cw[2:3, :] * cu
            conv_pieces.append(
                gates_scr[rows, OFF_B - OFF_C + p0:OFF_B - OFF_C + p0 + CONV_PIECE] * conv)
            if p0 + CONV_PIECE < CONV_WIDTH:
                yield
        conv = jnp.concatenate(conv_pieces, axis=1)
        mixed_scr[rows, ATTN_WIDTH:] = _rms(conv, gc_ref[...]).astype(bf16)
        mixed_scr[rows, :ATTN_WIDTH] = _rms(attn_scr[rows, :], ga_ref[...]).astype(bf16)
        yield

        o_ref[rows, :] = x + _dot(mixed_scr[rows, :], w_out_ref[...])
        yield

        z_scr[rows, :] = _rms(o_ref[rows, :], g2_ref[...]).astype(bf16)
        yield
        for cc in range(D_FF // FF_CHUNK):
            a = _dot(z_scr[rows, :], w1_ref[:, cc * FF_CHUNK:(cc + 1) * FF_CHUNK])
            a_scr[rows, :] = jnp.square(jnp.maximum(a, 0.0)).astype(bf16)
            yield
            o_ref[rows, :] += _dot(a_scr[rows, :], w2_ref[cc * FF_CHUNK:(cc + 1) * FF_CHUNK, :])
            yield

        o_ref[rows, :] = _rms(o_ref[rows, :], gf_ref[...])

    chains = {"A": chain(0), "B": chain(1)}
    finished = set()
    for name in _EMISSION:
        assert name not in finished
        if next(chains[name], "done") == "done":
            finished.add(name)
    assert finished == set(chains), "emission plan does not cover every stage"

    k_scr[:, 0:BLOCK, :] = k_scr[:, TILE:TILE + BLOCK, :]
    vt_scr[:, 0:BLOCK] = vt_scr[:, TILE:TILE + BLOCK]
    cu_scr[0:SUBLANES, :] = cu_scr[TILE:TILE + SUBLANES, :]


def _resident(shape):
    return pl.BlockSpec(shape, lambda b, t: (0,) * len(shape),
                        pipeline_mode=pl.Buffered(1))


@jax.jit
def kernel(x, norm1_g, w_in, conv_w, sinks, attn_norm_g, conv_norm_g, w_out, norm2_g,
           w_ff1, w_ff2, final_g):
    batch, seq, d = x.shape
    assert d == D_MODEL and seq % TILE == 0
    assert norm1_g.shape[0] == 1, "single layer"
    bf16 = jnp.bfloat16
    row = lambda v: v.reshape(1, -1).astype(jnp.float32)

    tile_spec = pl.BlockSpec((None, TILE, D_MODEL), lambda b, t: (b, t, 0))
    in_specs = [
        tile_spec,
        _resident((1, D_MODEL)),
        _resident((D_MODEL, IN_WIDTH)),
        _resident((CONV_K, CONV_WIDTH)),
        pl.BlockSpec(memory_space=pltpu.SMEM),
        _resident((1, ATTN_WIDTH)),
        _resident((1, CONV_WIDTH)),
        _resident((D_MODEL, D_MODEL)),
        _resident((1, D_MODEL)),
        _resident((D_MODEL, D_FF)),
        _resident((D_FF, D_MODEL)),
        _resident((1, D_MODEL)),
    ]
    scratch_shapes = [
        pltpu.VMEM((2, ATTN_HEADS, BLOCK, BLOCK), jnp.float32),
        pltpu.VMEM((TILE // BLOCK, ATTN_HEADS, BLOCK, LANES), bf16),
        pltpu.VMEM((KV_HEADS, BLOCK + TILE, LANES), bf16),
        pltpu.VMEM((KV_WIDTH, BLOCK + TILE), bf16),
        pltpu.VMEM((TILE, ATTN_WIDTH), jnp.float32),
        pltpu.VMEM((TILE, 3 * CONV_WIDTH), jnp.float32),
        pltpu.VMEM((SUBLANES + TILE, CONV_WIDTH), jnp.float32),
        pltpu.VMEM((TILE, D_MODEL), bf16),
        pltpu.VMEM((TILE, D_MODEL), bf16),
        pltpu.VMEM((TILE, D_MODEL), bf16),
        pltpu.VMEM((TILE, FF_CHUNK), bf16),
    ]
    return pl.pallas_call(
        _layer_kernel,
        grid=(batch, seq // TILE),
        in_specs=in_specs,
        out_specs=tile_spec,
        out_shape=jax.ShapeDtypeStruct(x.shape, x.dtype),
        scratch_shapes=scratch_shapes,
        compiler_params=pltpu.CompilerParams(
            dimension_semantics=("arbitrary", "arbitrary"),
            vmem_limit_bytes=VMEM_LIMIT_BYTES),
        name="hybrid_layer",
    )(x, row(norm1_g[0]), w_in[0].astype(bf16), conv_w[0].astype(jnp.float32),
      sinks[0].astype(jnp.float32), row(attn_norm_g[0]), row(conv_norm_g[0]),
      w_out[0].astype(bf16), row(norm2_g[0]), w_ff1[0].astype(bf16),
      w_ff2[0].astype(bf16), row(final_g))
```

```python
import jax
import jax.numpy as jnp
from jax import lax
from jax.experimental import pallas as pl
from jax.experimental.pallas import tpu as pltpu

D_MODEL = 1024
ATTN_HEADS = 8
KV_HEADS = 2
HEAD_DIM = 64
ATTN_WIDTH = ATTN_HEADS * HEAD_DIM
KV_WIDTH = KV_HEADS * HEAD_DIM
GROUP = ATTN_HEADS // KV_HEADS
WINDOW = 128
BLOCK = 128
CONV_WIDTH = D_MODEL - ATTN_WIDTH
CONV_K = 3
D_FF = 4 * D_MODEL
EPS = 1e-5
OFF_Q = 0
OFF_KV = ATTN_WIDTH
OFF_C = ATTN_WIDTH + 2 * KV_WIDTH
OFF_B = OFF_C + CONV_WIDTH
OFF_U = OFF_B + CONV_WIDTH
IN_WIDTH = OFF_U + CONV_WIDTH

LANES = 128
SUBLANES = 8
TILE = 1024
SUB = 512
N_SUBS = TILE // SUB
N_CHAINS = 2
CHAIN = SUB // N_CHAINS
CHAIN_BLOCKS = CHAIN // BLOCK
FF_CHUNK = 1024
GATE_TILE = 256
MASKED = -1e30
VMEM_LIMIT_BYTES = 60 * 1024 * 1024

assert WINDOW == BLOCK and HEAD_DIM * 2 == LANES and KV_WIDTH == LANES

CONV_PIECE = 256
_STAGES_PER_CHAIN = 19
_EMISSION = "AABBAAAABABABABAABABABABABABABBAABCBCBDBDBCCCCDCDCDCDCCDCDCDCDCDCDCDDCCDDDDD"
assert all(_EMISSION.count(ch) == _STAGES_PER_CHAIN for ch in "ABCD")


def _stage_positions(name):
    return [i for i, ch in enumerate(_EMISSION) if ch == name]


for _first, _second in ("AB", "BC", "CD"):
    for _a, _b in ((1, 2), (6, 6), (7, 7)):
        assert _stage_positions(_first)[_a] < _stage_positions(_second)[_b]
for _first, _second in ("AC", "BD"):
    for _w, _r in ((0, 5), (1, 5), (2, 7), (7, 8), (9, 16), (10, 17)):
        assert _stage_positions(_second)[_w] > _stage_positions(_first)[_r]


def _rms(xf, g):
    inv = lax.rsqrt(jnp.mean(xf * xf, axis=-1, keepdims=True) + EPS)
    return xf * inv * g


def _dot(a, b):
    return jnp.dot(a, b, preferred_element_type=jnp.float32)


def _layer_kernel(x_ref, g1_ref, w_in_ref, convw_ref, sinks_ref, ga_ref, gc_ref,
                  w_out_ref, g2_ref, w1_ref, w2_ref, gf_ref,
                  o_ref,
                  bias_scr, q_scr, k_scr, vt_scr, attn_scr, gates_scr, cu_scr, mixed_scr,
                  y_scr, z_scr, a_scr):
    t = pl.program_id(1)
    bf16 = jnp.bfloat16
    f32 = jnp.float32

    slot = lax.broadcasted_iota(jnp.int32, (BLOCK, BLOCK), 0)
    qpos = lax.broadcasted_iota(jnp.int32, (BLOCK, BLOCK), 1)
    is_cur = slot <= qpos

    @pl.when(t == 0)
    def _start_of_sequence():
        k_scr[:, 0:BLOCK, :] = jnp.zeros((KV_HEADS, BLOCK, LANES), bf16)
        vt_scr[:, 0:BLOCK] = jnp.zeros((KV_WIDTH, BLOCK), bf16)
        cu_scr[0:SUBLANES, :] = jnp.zeros((SUBLANES, CONV_WIDTH), f32)
        dist = jnp.where(is_cur, qpos - slot, qpos - slot + BLOCK).astype(f32)
        for h in range(ATTN_HEADS):
            slope = 2.0 ** (-8.0 * (h + 1) / ATTN_HEADS)
            b = -slope * dist
            bias_scr[0, h] = b
            bias_scr[1, h] = jnp.where(is_cur, b, MASKED)

    first = jnp.where(t == 0, 1, 0)
    low = lax.broadcasted_iota(jnp.int32, (CHAIN, LANES), 1) < HEAD_DIM
    zero = jnp.zeros((CHAIN, LANES), f32)
    gate_tiles = list(range(OFF_C, IN_WIDTH, GATE_TILE))

    def chain(sub, c):
        r0 = c * CHAIN
        rows = slice(r0, r0 + CHAIN)
        t0 = sub * SUB + r0
        trows = slice(t0, t0 + CHAIN)
        blocks = range(c * CHAIN_BLOCKS, (c + 1) * CHAIN_BLOCKS)
        tblock = lambda j: sub * (SUB // BLOCK) + j

        x = x_ref[trows, :]
        y_scr[rows, :] = _rms(x, g1_ref[...]).astype(bf16)
        yield

        q = _dot(y_scr[rows, :], w_in_ref[:, OFF_Q:OFF_Q + ATTN_WIDTH]) * (HEAD_DIM ** -0.5)
        for pair in range(ATTN_HEADS // 2):
            qp = q[:, pair * LANES:(pair + 1) * LANES]
            q_lo = jnp.where(low, qp, zero).astype(bf16)
            q_hi = jnp.where(low, zero, qp).astype(bf16)
            for jj, j in enumerate(blocks):
                q_scr[j, 2 * pair] = q_lo[jj * BLOCK:(jj + 1) * BLOCK]
                q_scr[j, 2 * pair + 1] = q_hi[jj * BLOCK:(jj + 1) * BLOCK]
        kv = _dot(y_scr[rows, :], w_in_ref[:, OFF_KV:OFF_KV + 2 * KV_WIDTH])
        k = kv[:, :KV_WIDTH]
        k_swapped = pltpu.roll(k, HEAD_DIM, axis=1)
        k_scr[0, BLOCK + t0:BLOCK + t0 + CHAIN, :] = jnp.where(low, k, k_swapped).astype(bf16)
        k_scr[1, BLOCK + t0:BLOCK + t0 + CHAIN, :] = jnp.where(low, k_swapped, k).astype(bf16)
        vt_scr[:, BLOCK + t0:BLOCK + t0 + CHAIN] = kv[:, KV_WIDTH:].T.astype(bf16)
        yield

        def gate_piece(i):
            if i < len(gate_tiles):
                c0 = gate_tiles[i]
                gates_scr[rows, c0 - OFF_C:c0 - OFF_C + GATE_TILE] = _dot(
                    y_scr[rows, :], w_in_ref[:, c0:c0 + GATE_TILE])

        def scores(j, g):
            kb = k_scr[g, tblock(j) * BLOCK:(tblock(j) + 2) * BLOCK, :]
            qs = q_scr[j, GROUP * g:GROUP * (g + 1)].reshape(GROUP * BLOCK, LANES)
            return lax.dot_general(kb, qs, (((1,), (1,)), ((), ())),
                                   preferred_element_type=f32)

        def softmax(j, g, st):
            table = first if tblock(j) == 0 else 0
            probs = []
            invs = []
            for hh in range(GROUP):
                h = GROUP * g + hh
                cols = slice(hh * BLOCK, (hh + 1) * BLOCK)
                logits = (jnp.where(is_cur, st[BLOCK:, cols], st[:BLOCK, cols])
                          + bias_scr[table, h])
                sink = sinks_ref[h]
                m = jnp.maximum(jnp.max(logits, axis=0, keepdims=True), sink)
                p = jnp.exp(logits - m)
                den = jnp.sum(p, axis=0, keepdims=True) + jnp.exp(sink - m)
                invs.append(1.0 / den)
                zeros = jnp.zeros_like(p)
                probs.append(jnp.concatenate(
                    [jnp.where(is_cur, zeros, p), jnp.where(is_cur, p, zeros)],
                    axis=0).astype(bf16))
            return jnp.concatenate(probs, axis=1), jnp.concatenate(invs, axis=1)

        def values(j, g, pt, inv):
            vt = vt_scr[g * HEAD_DIM:(g + 1) * HEAD_DIM,
                        tblock(j) * BLOCK:(tblock(j) + 2) * BLOCK]
            ot = _dot(vt, pt) * inv
            for hp in range(GROUP // 2):
                two = ot[:, 2 * hp * BLOCK:(2 * hp + 2) * BLOCK]
                pair = (GROUP // 2) * g + hp
                attn_scr[j * BLOCK:(j + 1) * BLOCK, pair * LANES:(pair + 1) * LANES] = (
                    jnp.concatenate([two[:, :BLOCK], two[:, BLOCK:]], axis=0).T)

        units = [(j, g) for j in blocks for g in range(KV_HEADS)]
        assert len(gate_tiles) <= len(units) + 2
        gate_piece(0)
        st = {0: scores(*units[0]), 1: scores(*units[1])}
        sm = {}
        for i, (j, g) in enumerate(units):
            sm[i] = softmax(j, g, st.pop(i))
            if i + 2 < len(units):
                st[i + 2] = scores(*units[i + 2])
            gate_piece(i + 1)
            if i >= 1:
                values(*units[i - 1], *sm.pop(i - 1))
            if i + 1 == len(units):
                gate_piece(i + 2)
                values(*units[i], *sm.pop(i))
            yield

        conv_pieces = []
        for p0 in range(0, CONV_WIDTH, CONV_PIECE):
            cols = slice(p0, p0 + CONV_PIECE)
            cu = (gates_scr[rows, OFF_C - OFF_C + p0:OFF_C - OFF_C + p0 + CONV_PIECE]
                  * gates_scr[rows, OFF_U - OFF_C + p0:OFF_U - OFF_C + p0 + CONV_PIECE])
            cu_scr[SUBLANES + t0:SUBLANES + t0 + CHAIN, cols] = cu
            cw = convw_ref[:, cols]
            ext = jnp.concatenate([cu_scr[t0:t0 + SUBLANES, cols], cu], axis=0)
            prev1 = pltpu.roll(ext, 1, axis=0)[SUBLANES:]
            prev2 = pltpu.roll(ext, 2, axis=0)[SUBLANES:]
            conv = cw[0:1, :] * prev2 + cw[1:2, :] * prev1 + cw[2:3, :] * cu
            conv_pieces.append(
                gates_scr[rows, OFF_B - OFF_C + p0:OFF_B - OFF_C + p0 + CONV_PIECE] * conv)
            if p0 + CONV_PIECE < CONV_WIDTH:
                yield
        conv = jnp.concatenate(conv_pieces, axis=1)
        mixed_scr[rows, ATTN_WIDTH:] = _rms(conv, gc_ref[...]).astype(bf16)
        mixed_scr[rows, :ATTN_WIDTH] = _rms(attn_scr[rows, :], ga_ref[...]).astype(bf16)
        yield

        o_ref[trows, :] = x + _dot(mixed_scr[rows, :], w_out_ref[...])
        yield

        z_scr[rows, :] = _rms(o_ref[trows, :], g2_ref[...]).astype(bf16)
        yield
        for cc in range(D_FF // FF_CHUNK):
            a = _dot(z_scr[rows, :], w1_ref[:, cc * FF_CHUNK:(cc + 1) * FF_CHUNK])
            a_scr[rows, :] = jnp.square(jnp.maximum(a, 0.0)).astype(bf16)
            yield
            o_ref[trows, :] += _dot(a_scr[rows, :], w2_ref[cc * FF_CHUNK:(cc + 1) * FF_CHUNK, :])
            yield

        o_ref[trows, :] = _rms(o_ref[trows, :], gf_ref[...])

    chains = {"A": chain(0, 0), "B": chain(0, 1), "C": chain(1, 0), "D": chain(1, 1)}
    finished = set()
    for name in _EMISSION:
        assert name not in finished
        if next(chains[name], "done") == "done":
            finished.add(name)
    assert finished == set(chains), "emission plan does not cover every stage"

    k_scr[:, 0:BLOCK, :] = k_scr[:, TILE:TILE + BLOCK, :]
    vt_scr[:, 0:BLOCK] = vt_scr[:, TILE:TILE + BLOCK]
    cu_scr[0:SUBLANES, :] = cu_scr[TILE:TILE + SUBLANES, :]


def _resident(shape):
    return pl.BlockSpec(shape, lambda b, t: (0,) * len(shape),
                        pipeline_mode=pl.Buffered(1))


@jax.jit
def kernel(x, norm1_g, w_in, conv_w, sinks, attn_norm_g, conv_norm_g, w_out, norm2_g,
           w_ff1, w_ff2, final_g):
    batch, seq, d = x.shape
    assert d == D_MODEL and seq % TILE == 0
    assert norm1_g.shape[0] == 1, "single layer"
    bf16 = jnp.bfloat16
    row = lambda v: v.reshape(1, -1).astype(jnp.float32)

    tile_spec = pl.BlockSpec((None, TILE, D_MODEL), lambda b, t: (b, t, 0))
    in_specs = [
        tile_spec,
        _resident((1, D_MODEL)),
        _resident((D_MODEL, IN_WIDTH)),
        _resident((CONV_K, CONV_WIDTH)),
        pl.BlockSpec(memory_space=pltpu.SMEM),
        _resident((1, ATTN_WIDTH)),
        _resident((1, CONV_WIDTH)),
        _resident((D_MODEL, D_MODEL)),
        _resident((1, D_MODEL)),
        _resident((D_MODEL, D_FF)),
        _resident((D_FF, D_MODEL)),
        _resident((1, D_MODEL)),
    ]
    scratch_shapes = [
        pltpu.VMEM((2, ATTN_HEADS, BLOCK, BLOCK), jnp.float32),
        pltpu.VMEM((SUB // BLOCK, ATTN_HEADS, BLOCK, LANES), bf16),
        pltpu.VMEM((KV_HEADS, BLOCK + TILE, LANES), bf16),
        pltpu.VMEM((KV_WIDTH, BLOCK + TILE), bf16),
        pltpu.VMEM((SUB, ATTN_WIDTH), jnp.float32),
        pltpu.VMEM((SUB, 3 * CONV_WIDTH), jnp.float32),
        pltpu.VMEM((SUBLANES + TILE, CONV_WIDTH), jnp.float32),
        pltpu.VMEM((SUB, D_MODEL), bf16),
        pltpu.VMEM((SUB, D_MODEL), bf16),
        pltpu.VMEM((SUB, D_MODEL), bf16),
        pltpu.VMEM((SUB, FF_CHUNK), bf16),
    ]
    return pl.pallas_call(
        _layer_kernel,
        grid=(batch, seq // TILE),
        in_specs=in_specs,
        out_specs=tile_spec,
        out_shape=jax.ShapeDtypeStruct(x.shape, x.dtype),
        scratch_shapes=scratch_shapes,
        compiler_params=pltpu.CompilerParams(
            dimension_semantics=("arbitrary", "arbitrary"),
            vmem_limit_bytes=VMEM_LIMIT_BYTES),
        name="hybrid_layer",
    )(x, row(norm1_g[0]), w_in[0].astype(bf16), conv_w[0].astype(jnp.float32),
      sinks[0].astype(jnp.float32), row(attn_norm_g[0]), row(conv_norm_g[0]),
      w_out[0].astype(bf16), row(norm2_g[0]), w_ff1[0].astype(bf16),
      w_ff2[0].astype(bf16), row(final_g))
```

```python
import jax
import jax.numpy as jnp
from jax import lax
from jax.experimental import pallas as pl
from jax.experimental.pallas import tpu as pltpu

D_MODEL = 1024
ATTN_HEADS = 8
KV_HEADS = 2
HEAD_DIM = 64
ATTN_WIDTH = ATTN_HEADS * HEAD_DIM
KV_WIDTH = KV_HEADS * HEAD_DIM
GROUP = ATTN_HEADS // KV_HEADS
WINDOW = 128
BLOCK = 128
CONV_WIDTH = D_MODEL - ATTN_WIDTH
CONV_K = 3
D_FF = 4 * D_MODEL
EPS = 1e-5
OFF_Q = 0
OFF_KV = ATTN_WIDTH
OFF_C = ATTN_WIDTH + 2 * KV_WIDTH
OFF_B = OFF_C + CONV_WIDTH
OFF_U = OFF_B + CONV_WIDTH
IN_WIDTH = OFF_U + CONV_WIDTH

LANES = 128
SUBLANES = 8
TILE = 1024
SUB = 512
N_SUBS = TILE // SUB
N_CHAINS = 2
CHAIN = SUB // N_CHAINS
CHAIN_BLOCKS = CHAIN // BLOCK
FF_CHUNK = 1024
GATE_TILE = 256
MASKED = -1e30
VMEM_LIMIT_BYTES = 60 * 1024 * 1024

assert WINDOW == BLOCK and HEAD_DIM * 2 == LANES and KV_WIDTH == LANES

CONV_PIECE = 256
_STAGES_PER_CHAIN = 19
_EMISSION = "AABBAAAABABABABAABABABABABABABBAABCCBCBDBDBCCCDCDCDCDCCDCDCDCDCDCDCDDCCDDDDD"
assert all(_EMISSION.count(ch) == _STAGES_PER_CHAIN for ch in "ABCD")


def _stage_positions(name):
    return [i for i, ch in enumerate(_EMISSION) if ch == name]


for _first, _second in ("AB", "BC", "CD"):
    for _a, _b in ((1, 2), (6, 6), (7, 7)):
        assert _stage_positions(_first)[_a] < _stage_positions(_second)[_b]
for _first, _second in ("AC", "BD"):
    for _w, _r in ((0, 5), (1, 5), (2, 7), (7, 8), (9, 16), (10, 17)):
        assert _stage_positions(_second)[_w] > _stage_positions(_first)[_r]


def _rms(xf, g):
    inv = lax.rsqrt(jnp.mean(xf * xf, axis=-1, keepdims=True) + EPS)
    return xf * inv * g


def _dot(a, b):
    return jnp.dot(a, b, preferred_element_type=jnp.float32)


def _layer_kernel(x_ref, g1_ref, w_in_ref, convw_ref, sinks_ref, ga_ref, gc_ref,
                  w_out_ref, g2_ref, w1_ref, w2_ref, gf_ref,
                  o_ref,
                  bias_scr, q_scr, k_scr, vt_scr, attn_scr, gates_scr, cu_scr, mixed_scr,
                  y_scr, z_scr, a_scr):
    t = pl.program_id(1)
    bf16 = jnp.bfloat16
    f32 = jnp.float32

    slot = lax.broadcasted_iota(jnp.int32, (BLOCK, BLOCK), 0)
    qpos = lax.broadcasted_iota(jnp.int32, (BLOCK, BLOCK), 1)
    is_cur = slot <= qpos

    @pl.when(t == 0)
    def _start_of_sequence():
        k_scr[:, 0:BLOCK, :] = jnp.zeros((KV_HEADS, BLOCK, LANES), bf16)
        vt_scr[:, 0:BLOCK] = jnp.zeros((KV_WIDTH, BLOCK), bf16)
        cu_scr[0:SUBLANES, :] = jnp.zeros((SUBLANES, CONV_WIDTH), f32)
        dist = jnp.where(is_cur, qpos - slot, qpos - slot + BLOCK).astype(f32)
        for h in range(ATTN_HEADS):
            slope = 2.0 ** (-8.0 * (h + 1) / ATTN_HEADS)
            b = -slope * dist
            bias_scr[0, h] = b
            bias_scr[1, h] = jnp.where(is_cur, b, MASKED)

    first = jnp.where(t == 0, 1, 0)
    low = lax.broadcasted_iota(jnp.int32, (CHAIN, LANES), 1) < HEAD_DIM
    zero = jnp.zeros((CHAIN, LANES), f32)
    gate_tiles = list(range(OFF_C, IN_WIDTH, GATE_TILE))

    def chain(sub, c):
        r0 = c * CHAIN
        rows = slice(r0, r0 + CHAIN)
        t0 = sub * SUB + r0
        trows = slice(t0, t0 + CHAIN)
        blocks = range(c * CHAIN_BLOCKS, (c + 1) * CHAIN_BLOCKS)
        tblock = lambda j: sub * (SUB // BLOCK) + j

        x = x_ref[trows, :]
        y_scr[rows, :] = _rms(x, g1_ref[...]).astype(bf16)
        yield

        q = _dot(y_scr[rows, :], w_in_ref[:, OFF_Q:OFF_Q + ATTN_WIDTH]) * (HEAD_DIM ** -0.5)
        for pair in range(ATTN_HEADS // 2):
            qp = q[:, pair * LANES:(pair + 1) * LANES]
            q_lo = jnp.where(low, qp, zero).astype(bf16)
            q_hi = jnp.where(low, zero, qp).astype(bf16)
            for jj, j in enumerate(blocks):
                q_scr[j, 2 * pair] = q_lo[jj * BLOCK:(jj + 1) * BLOCK]
                q_scr[j, 2 * pair + 1] = q_hi[jj * BLOCK:(jj + 1) * BLOCK]
        kv = _dot(y_scr[rows, :], w_in_ref[:, OFF_KV:OFF_KV + 2 * KV_WIDTH])
        k = kv[:, :KV_WIDTH]
        k_swapped = pltpu.roll(k, HEAD_DIM, axis=1)
        k_scr[0, BLOCK + t0:BLOCK + t0 + CHAIN, :] = jnp.where(low, k, k_swapped).astype(bf16)
        k_scr[1, BLOCK + t0:BLOCK + t0 + CHAIN, :] = jnp.where(low, k_swapped, k).astype(bf16)
        vt_scr[:, BLOCK + t0:BLOCK + t0 + CHAIN] = kv[:, KV_WIDTH:].T.astype(bf16)
        yield

        def gate_piece(i):
            if i < len(gate_tiles):
                c0 = gate_tiles[i]
                gates_scr[rows, c0 - OFF_C:c0 - OFF_C + GATE_TILE] = _dot(
                    y_scr[rows, :], w_in_ref[:, c0:c0 + GATE_TILE])

        def scores(j, g):
            kb = k_scr[g, tblock(j) * BLOCK:(tblock(j) + 2) * BLOCK, :]
            qs = q_scr[j, GROUP * g:GROUP * (g + 1)].reshape(GROUP * BLOCK, LANES)
            return lax.dot_general(kb, qs, (((1,), (1,)), ((), ())),
                                   preferred_element_type=f32)

        def softmax(j, g, st):
            table = first if tblock(j) == 0 else 0
            probs = []
            invs = []
            for hh in range(GROUP):
                h = GROUP * g + hh
                cols = slice(hh * BLOCK, (hh + 1) * BLOCK)
                logits = (jnp.where(is_cur, st[BLOCK:, cols], st[:BLOCK, cols])
                          + bias_scr[table, h])
                sink = sinks_ref[h]
                m = jnp.maximum(jnp.max(logits, axis=0, keepdims=True), sink)
                p = jnp.exp(logits - m)
                den = jnp.sum(p, axis=0, keepdims=True) + jnp.exp(sink - m)
                invs.append(1.0 / den)
                zeros = jnp.zeros_like(p)
                probs.append(jnp.concatenate(
                    [jnp.where(is_cur, zeros, p), jnp.where(is_cur, p, zeros)],
                    axis=0).astype(bf16))
            return jnp.concatenate(probs, axis=1), jnp.concatenate(invs, axis=1)

        def values(j, g, pt, inv):
            vt = vt_scr[g * HEAD_DIM:(g + 1) * HEAD_DIM,
                        tblock(j) * BLOCK:(tblock(j) + 2) * BLOCK]
            ot = _dot(vt, pt) * inv
            for hp in range(GROUP // 2):
                two = ot[:, 2 * hp * BLOCK:(2 * hp + 2) * BLOCK]
                pair = (GROUP // 2) * g + hp
                attn_scr[j * BLOCK:(j + 1) * BLOCK, pair * LANES:(pair + 1) * LANES] = (
                    jnp.concatenate([two[:, :BLOCK], two[:, BLOCK:]], axis=0).T)

        units = [(j, g) for j in blocks for g in range(KV_HEADS)]
        assert len(gate_tiles) <= len(units) + 2
        gate_piece(0)
        st = {0: scores(*units[0]), 1: scores(*units[1])}
        sm = {}
        for i, (j, g) in enumerate(units):
            sm[i] = softmax(j, g, st.pop(i))
            if i + 2 < len(units):
                st[i + 2] = scores(*units[i + 2])
            gate_piece(i + 1)
            if i >= 1:
                values(*units[i - 1], *sm.pop(i - 1))
            if i + 1 == len(units):
                gate_piece(i + 2)
                values(*units[i], *sm.pop(i))
            yield

        conv_pieces = []
        for p0 in range(0, CONV_WIDTH, CONV_PIECE):
            cols = slice(p0, p0 + CONV_PIECE)
            cu = (gates_scr[rows, OFF_C - OFF_C + p0:OFF_C - OFF_C + p0 + CONV_PIECE]
                  * gates_scr[rows, OFF_U - OFF_C + p0:OFF_U - OFF_C + p0 + CONV_PIECE])
            cu_scr[SUBLANES + t0:SUBLANES + t0 + CHAIN, cols] = cu
            cw = convw_ref[:, cols]
            ext = jnp.concatenate([cu_scr[t0:t0 + SUBLANES, cols], cu], axis=0)
            prev1 = pltpu.roll(ext, 1, axis=0)[SUBLANES:]
            prev2 = pltpu.roll(ext, 2, axis=0)[SUBLANES:]
            conv = cw[0:1, :] * prev2 + cw[1:2, :] * prev1 + cw[2:3, :] * cu
            conv_pieces.append(
                gates_scr[rows, OFF_B - OFF_C + p0:OFF_B - OFF_C + p0 + CONV_PIECE] * conv)
            if p0 + CONV_PIECE < CONV_WIDTH:
                yield
        conv = jnp.concatenate(conv_pieces, axis=1)
        mixed_scr[rows, ATTN_WIDTH:] = _rms(conv, gc_ref[...]).astype(bf16)
        mixed_scr[rows, :ATTN_WIDTH] = _rms(attn_scr[rows, :], ga_ref[...]).astype(bf16)
        yield

        o_ref[trows, :] = x + _dot(mixed_scr[rows, :], w_out_ref[...])
        yield

        z_scr[rows, :] = _rms(o_ref[trows, :], g2_ref[...]).astype(bf16)
        yield
        for cc in range(D_FF // FF_CHUNK):
            a = _dot(z_scr[rows, :], w1_ref[:, cc * FF_CHUNK:(cc + 1) * FF_CHUNK])
            a_scr[rows, :] = jnp.square(jnp.maximum(a, 0.0)).astype(bf16)
            yield
            o_ref[trows, :] += _dot(a_scr[rows, :], w2_ref[cc * FF_CHUNK:(cc + 1) * FF_CHUNK, :])
            yield

        o_ref[trows, :] = _rms(o_ref[trows, :], gf_ref[...])

    chains = {"A": chain(0, 0), "B": chain(0, 1), "C": chain(1, 0), "D": chain(1, 1)}
    finished = set()
    for name in _EMISSION:
        assert name not in finished
        if next(chains[name], "done") == "done":
            finished.add(name)
    assert finished == set(chains), "emission plan does not cover every stage"

    k_scr[:, 0:BLOCK, :] = k_scr[:, TILE:TILE + BLOCK, :]
    vt_scr[:, 0:BLOCK] = vt_scr[:, TILE:TILE + BLOCK]
    cu_scr[0:SUBLANES, :] = cu_scr[TILE:TILE + SUBLANES, :]


def _resident(shape):
    return pl.BlockSpec(shape, lambda b, t: (0,) * len(shape),
                        pipeline_mode=pl.Buffered(1))


@jax.jit
def kernel(x, norm1_g, w_in, conv_w, sinks, attn_norm_g, conv_norm_g, w_out, norm2_g,
           w_ff1, w_ff2, final_g):
    batch, seq, d = x.shape
    assert d == D_MODEL and seq % TILE == 0
    assert norm1_g.shape[0] == 1, "single layer"
    bf16 = jnp.bfloat16
    row = lambda v: v.reshape(1, -1).astype(jnp.float32)

    tile_spec = pl.BlockSpec((None, TILE, D_MODEL), lambda b, t: (b, t, 0))
    in_specs = [
        tile_spec,
        _resident((1, D_MODEL)),
        _resident((D_MODEL, IN_WIDTH)),
        _resident((CONV_K, CONV_WIDTH)),
        pl.BlockSpec(memory_space=pltpu.SMEM),
        _resident((1, ATTN_WIDTH)),
        _resident((1, CONV_WIDTH)),
        _resident((D_MODEL, D_MODEL)),
        _resident((1, D_MODEL)),
        _resident((D_MODEL, D_FF)),
        _resident((D_FF, D_MODEL)),
        _resident((1, D_MODEL)),
    ]
    scratch_shapes = [
        pltpu.VMEM((2, ATTN_HEADS, BLOCK, BLOCK), jnp.float32),
        pltpu.VMEM((SUB // BLOCK, ATTN_HEADS, BLOCK, LANES), bf16),
        pltpu.VMEM((KV_HEADS, BLOCK + TILE, LANES), bf16),
        pltpu.VMEM((KV_WIDTH, BLOCK + TILE), bf16),
        pltpu.VMEM((SUB, ATTN_WIDTH), jnp.float32),
        pltpu.VMEM((SUB, 3 * CONV_WIDTH), jnp.float32),
        pltpu.VMEM((SUBLANES + TILE, CONV_WIDTH), jnp.float32),
        pltpu.VMEM((SUB, D_MODEL), bf16),
        pltpu.VMEM((SUB, D_MODEL), bf16),
        pltpu.VMEM((SUB, D_MODEL), bf16),
        pltpu.VMEM((SUB, FF_CHUNK), bf16),
    ]
    return pl.pallas_call(
        _layer_kernel,
        grid=(batch, seq // TILE),
        in_specs=in_specs,
        out_specs=tile_spec,
        out_shape=jax.ShapeDtypeStruct(x.shape, x.dtype),
        scratch_shapes=scratch_shapes,
        compiler_params=pltpu.CompilerParams(
            dimension_semantics=("arbitrary", "arbitrary"),
            vmem_limit_bytes=VMEM_LIMIT_BYTES),
        name="hybrid_layer",
    )(x, row(norm1_g[0]), w_in[0].astype(bf16), conv_w[0].astype(jnp.float32),
      sinks[0].astype(jnp.float32), row(attn_norm_g[0]), row(conv_norm_g[0]),
      w_out[0].astype(bf16), row(norm2_g[0]), w_ff1[0].astype(bf16),
      w_ff2[0].astype(bf16), row(final_g))
```

```python
import jax
import jax.numpy as jnp
from jax import lax
from jax.experimental import pallas as pl
from jax.experimental.pallas import tpu as pltpu

D_MODEL = 1024
ATTN_HEADS = 8
KV_HEADS = 2
HEAD_DIM = 64
ATTN_WIDTH = ATTN_HEADS * HEAD_DIM
KV_WIDTH = KV_HEADS * HEAD_DIM
GROUP = ATTN_HEADS // KV_HEADS
WINDOW = 128
BLOCK = 128
HALF = BLOCK // 2
CONV_WIDTH = D_MODEL - ATTN_WIDTH
CONV_K = 3
D_FF = 4 * D_MODEL
EPS = 1e-5
OFF_Q = 0
OFF_KV = ATTN_WIDTH
OFF_C = ATTN_WIDTH + 2 * KV_WIDTH
OFF_B = OFF_C + CONV_WIDTH
OFF_U = OFF_B + CONV_WIDTH
IN_WIDTH = OFF_U + CONV_WIDTH

LANES = 128
SUBLANES = 8
TILE = 1024
SUB = 512
N_SUBS = TILE // SUB
N_CHAINS = 2
CHAIN = SUB // N_CHAINS
CHAIN_BLOCKS = CHAIN // BLOCK
FF_CHUNK = 1024
GATE_TILE = 256
MASKED = -1e30
VMEM_LIMIT_BYTES = 60 * 1024 * 1024

assert WINDOW == BLOCK and HEAD_DIM * 2 == LANES and KV_WIDTH == LANES
assert N_SUBS == 2 and N_CHAINS == 2

CONV_PIECE = 256
_STAGES_PER_CHAIN = 19
_EMISSION = "AABBAAAABABABABAABABABABABABABBAABCCBCBDBDBCCCDCDCDCDCCDCDCDCDCDCDCDDCCDDDDD"
assert all(_EMISSION.count(ch) == _STAGES_PER_CHAIN for ch in "ABCD")


def _stage_positions(name):
    return [i for i, ch in enumerate(_EMISSION) if ch == name]


for _first, _second in ("AB", "BC", "CD"):
    for _a, _b in ((1, 2), (6, 6), (7, 7)):
        assert _stage_positions(_first)[_a] < _stage_positions(_second)[_b]
for _first, _second in ("AC", "BD"):
    for _w, _r in ((0, 5), (1, 5), (2, 7), (7, 8), (9, 16), (10, 17)):
        assert _stage_positions(_second)[_w] > _stage_positions(_first)[_r]


def _rms(xf, g):
    inv = lax.rsqrt(jnp.mean(xf * xf, axis=-1, keepdims=True) + EPS)
    return xf * inv * g


def _dot(a, b):
    return jnp.dot(a, b, preferred_element_type=jnp.float32)


def _layer_kernel(x_ref, g1_ref, w_in_ref, convw_ref, sinks_ref, ga_ref, gc_ref,
                  w_out_ref, g2_ref, w1_ref, w2_ref, gf_ref,
                  o_ref,
                  bias_scr, q_scr, k_scr, vt_scr, attn_scr, gates_scr, cu_scr, mixed_scr,
                  y_scr, z_scr, a_scr):
    t = pl.program_id(1)
    bf16 = jnp.bfloat16
    f32 = jnp.float32

    col = lax.broadcasted_iota(jnp.int32, (1, GROUP * BLOCK), 1)
    head_col = (col // HALF) % GROUP
    qcol = (col // (GROUP * HALF)) * HALF + col % HALF
    slot_cols = lax.broadcasted_iota(jnp.int32, (BLOCK, GROUP * BLOCK), 0)
    cur_cols = slot_cols <= qcol
    tri = (lax.broadcasted_iota(jnp.int32, (HALF, GROUP * HALF), 0)
           <= lax.broadcasted_iota(jnp.int32, (HALF, GROUP * HALF), 1) % HALF)
    low_half = lax.broadcasted_iota(jnp.int32, (HEAD_DIM, LANES), 1) < HALF

    @pl.when(t == 0)
    def _start_of_sequence():
        k_scr[:, 0:BLOCK, :] = jnp.zeros((KV_HEADS, BLOCK, LANES), bf16)
        vt_scr[:, 0:BLOCK] = jnp.zeros((KV_WIDTH, BLOCK), bf16)
        cu_scr[0:SUBLANES, :] = jnp.zeros((SUBLANES, CONV_WIDTH), f32)
        dist = jnp.where(cur_cols, qcol - slot_cols, qcol - slot_cols + BLOCK).astype(f32)
        for g in range(KV_HEADS):
            slope = jnp.zeros((1, GROUP * BLOCK), f32)
            for hh in range(GROUP):
                slope = jnp.where(head_col == hh, 2.0 ** (-8.0 * (GROUP * g + hh + 1) / ATTN_HEADS), slope)
            b = -slope * dist
            bias_scr[0, g] = b
            bias_scr[1, g] = jnp.where(cur_cols, b, MASKED)

    first = jnp.where(t == 0, 1, 0)
    low = lax.broadcasted_iota(jnp.int32, (CHAIN, LANES), 1) < HEAD_DIM
    zero = jnp.zeros((CHAIN, LANES), f32)
    gate_tiles = list(range(OFF_C, IN_WIDTH, GATE_TILE))

    def chain(sub, c):
        r0 = c * CHAIN
        rows = slice(r0, r0 + CHAIN)
        t0 = sub * SUB + r0
        trows = slice(t0, t0 + CHAIN)
        blocks = range(c * CHAIN_BLOCKS, (c + 1) * CHAIN_BLOCKS)
        tblock = lambda j: sub * (SUB // BLOCK) + j

        x = x_ref[trows, :]
        y_scr[rows, :] = _rms(x, g1_ref[...]).astype(bf16)
        yield

        q = _dot(y_scr[rows, :], w_in_ref[:, OFF_Q:OFF_Q + ATTN_WIDTH]) * (HEAD_DIM ** -0.5)
        for pair in range(ATTN_HEADS // 2):
            qp = q[:, pair * LANES:(pair + 1) * LANES]
            g, hh = divmod(2 * pair, GROUP)
            for r, masked in enumerate((jnp.where(low, qp, zero), jnp.where(low, zero, qp))):
                masked = masked.astype(bf16)
                for jj, j in enumerate(blocks):
                    for half in range(2):
                        q0 = jj * BLOCK + half * HALF
                        q_scr[j, g, half, hh + r] = masked[q0:q0 + HALF]
        kv = _dot(y_scr[rows, :], w_in_ref[:, OFF_KV:OFF_KV + 2 * KV_WIDTH])
        k = kv[:, :KV_WIDTH]
        k_swapped = pltpu.roll(k, HEAD_DIM, axis=1)
        k_scr[0, BLOCK + t0:BLOCK + t0 + CHAIN, :] = jnp.where(low, k, k_swapped).astype(bf16)
        k_scr[1, BLOCK + t0:BLOCK + t0 + CHAIN, :] = jnp.where(low, k_swapped, k).astype(bf16)
        vt_scr[:, BLOCK + t0:BLOCK + t0 + CHAIN] = kv[:, KV_WIDTH:].T.astype(bf16)
        yield

        def gate_piece(i):
            if i < len(gate_tiles):
                c0 = gate_tiles[i]
                gates_scr[rows, c0 - OFF_C:c0 - OFF_C + GATE_TILE] = _dot(
                    y_scr[rows, :], w_in_ref[:, c0:c0 + GATE_TILE])

        def scores(j, g):
            band = k_scr[g, tblock(j) * BLOCK:(tblock(j) + 2) * BLOCK, :]
            nt = (((1,), (1,)), ((), ()))
            q_a = q_scr[j, g, 0].reshape(GROUP * HALF, LANES)
            q_b = q_scr[j, g, 1].reshape(GROUP * HALF, LANES)
            s_a = lax.dot_general(band[:BLOCK + HALF], q_a, nt, preferred_element_type=f32)
            s_b = lax.dot_general(band[HALF:], q_b, nt, preferred_element_type=f32)
            return s_a, s_b

        def softmax(j, g, st):
            s_a, s_b = st
            table = first if tblock(j) == 0 else 0
            logits = jnp.concatenate([
                jnp.concatenate([jnp.where(tri, s_a[BLOCK:], s_a[:HALF]), s_a[HALF:BLOCK]], axis=0),
                jnp.concatenate([s_b[HALF:BLOCK], jnp.where(tri, s_b[BLOCK:], s_b[:HALF])], axis=0),
            ], axis=1) + bias_scr[table, g]
            sink = jnp.zeros((1, GROUP * BLOCK), f32)
            for hh in range(GROUP):
                sink = jnp.where(head_col == hh, sinks_ref[GROUP * g + hh], sink)
            m = jnp.maximum(jnp.max(logits, axis=0, keepdims=True), sink)
            p = jnp.exp(logits - m)
            den = jnp.sum(p, axis=0, keepdims=True) + jnp.exp(sink - m)
            p_a, p_b = p[:, :GROUP * HALF], p[:, GROUP * HALF:]
            zeros = jnp.zeros((HALF, GROUP * HALF), f32)
            pt_a = jnp.concatenate([jnp.where(tri, zeros, p_a[:HALF]), p_a[HALF:],
                                    jnp.where(tri, p_a[:HALF], zeros)], axis=0).astype(bf16)
            pt_b = jnp.concatenate([jnp.where(tri, zeros, p_b[HALF:]), p_b[:HALF],
                                    jnp.where(tri, p_b[HALF:], zeros)], axis=0).astype(bf16)
            return pt_a, pt_b, 1.0 / den

        def values(j, g, pt_a, pt_b, inv):
            vt = vt_scr[g * HEAD_DIM:(g + 1) * HEAD_DIM,
                        tblock(j) * BLOCK:(tblock(j) + 2) * BLOCK]
            o_a = _dot(vt[:, :BLOCK + HALF], pt_a) * inv[:, :GROUP * HALF]
            o_b = _dot(vt[:, HALF:], pt_b) * inv[:, GROUP * HALF:]
            for hp in range(GROUP // 2):
                a0 = o_a[:, hp * LANES:(hp + 1) * LANES]
                a1 = o_b[:, hp * LANES:(hp + 1) * LANES]
                even = jnp.where(low_half, a0, pltpu.roll(a1, HALF, axis=1))
                odd = jnp.where(low_half, pltpu.roll(a0, HALF, axis=1), a1)
                pair = (GROUP // 2) * g + hp
                attn_scr[j * BLOCK:(j + 1) * BLOCK, pair * LANES:(pair + 1) * LANES] = (
                    jnp.concatenate([even, odd], axis=0).T)

        units = [(j, g) for j in blocks for g in range(KV_HEADS)]
        assert len(gate_tiles) <= len(units) + 2
        gate_piece(0)
        st = {0: scores(*units[0]), 1: scores(*units[1])}
        sm = {}
        for i, (j, g) in enumerate(units):
            sm[i] = softmax(j, g, st.pop(i))
            if i + 2 < len(units):
                st[i + 2] = scores(*units[i + 2])
            gate_piece(i + 1)
            if i >= 1:
                values(*units[i - 1], *sm.pop(i - 1))
            if i + 1 == len(units):
                gate_piece(i + 2)
                values(*units[i], *sm.pop(i))
            yield

        conv_pieces = []
        for p0 in range(0, CONV_WIDTH, CONV_PIECE):
            cols = slice(p0, p0 + CONV_PIECE)
            cu = (gates_scr[rows, OFF_C - OFF_C + p0:OFF_C - OFF_C + p0 + CONV_PIECE]
                  * gates_scr[rows, OFF_U - OFF_C + p0:OFF_U - OFF_C + p0 + CONV_PIECE])
            cu_scr[SUBLANES + t0:SUBLANES + t0 + CHAIN, cols] = cu
            cw = convw_ref[:, cols]
            ext = jnp.concatenate([cu_scr[t0:t0 + SUBLANES, cols], cu], axis=0)
            prev1 = pltpu.roll(ext, 1, axis=0)[SUBLANES:]
            prev2 = pltpu.roll(ext, 2, axis=0)[SUBLANES:]
            conv = cw[0:1, :] * prev2 + cw[1:2, :] * prev1 + cw[2:3, :] * cu
            conv_pieces.append(
                gates_scr[rows, OFF_B - OFF_C + p0:OFF_B - OFF_C + p0 + CONV_PIECE] * conv)
            if p0 + CONV_PIECE < CONV_WIDTH:
                yield
        conv = jnp.concatenate(conv_pieces, axis=1)
        mixed_scr[rows, ATTN_WIDTH:] = _rms(conv, gc_ref[...]).astype(bf16)
        mixed_scr[rows, :ATTN_WIDTH] = _rms(attn_scr[rows, :], ga_ref[...]).astype(bf16)
        yield

        o_ref[trows, :] = x + _dot(mixed_scr[rows, :], w_out_ref[...])
        yield

        z_scr[rows, :] = _rms(o_ref[trows, :], g2_ref[...]).astype(bf16)
        yield
        for cc in range(D_FF // FF_CHUNK):
            a = _dot(z_scr[rows, :], w1_ref[:, cc * FF_CHUNK:(cc + 1) * FF_CHUNK])
            a_scr[rows, :] = jnp.square(jnp.maximum(a, 0.0)).astype(bf16)
            yield
            o_ref[trows, :] += _dot(a_scr[rows, :], w2_ref[cc * FF_CHUNK:(cc + 1) * FF_CHUNK, :])
            yield

        o_ref[trows, :] = _rms(o_ref[trows, :], gf_ref[...])

    chains = {"A": chain(0, 0), "B": chain(0, 1), "C": chain(1, 0), "D": chain(1, 1)}
    finished = set()
    for name in _EMISSION:
        assert name not in finished
        if next(chains[name], "done") == "done":
            finished.add(name)
    assert finished == set(chains), "emission plan does not cover every stage"

    k_scr[:, 0:BLOCK, :] = k_scr[:, TILE:TILE + BLOCK, :]
    vt_scr[:, 0:BLOCK] = vt_scr[:, TILE:TILE + BLOCK]
    cu_scr[0:SUBLANES, :] = cu_scr[TILE:TILE + SUBLANES, :]


def _resident(shape):
    return pl.BlockSpec(shape, lambda b, t: (0,) * len(shape),
                        pipeline_mode=pl.Buffered(1))


@jax.jit
def kernel(x, norm1_g, w_in, conv_w, sinks, attn_norm_g, conv_norm_g, w_out, norm2_g,
           w_ff1, w_ff2, final_g):
    batch, seq, d = x.shape
    assert d == D_MODEL and seq % TILE == 0
    assert norm1_g.shape[0] == 1, "single layer"
    bf16 = jnp.bfloat16
    row = lambda v: v.reshape(1, -1).astype(jnp.float32)

    tile_spec = pl.BlockSpec((None, TILE, D_MODEL), lambda b, t: (b, t, 0))
    in_specs = [
        tile_spec,
        _resident((1, D_MODEL)),
        _resident((D_MODEL, IN_WIDTH)),
        _resident((CONV_K, CONV_WIDTH)),
        pl.BlockSpec(memory_space=pltpu.SMEM),
        _resident((1, ATTN_WIDTH)),
        _resident((1, CONV_WIDTH)),
        _resident((D_MODEL, D_MODEL)),
        _resident((1, D_MODEL)),
        _resident((D_MODEL, D_FF)),
        _resident((D_FF, D_MODEL)),
        _resident((1, D_MODEL)),
    ]
    scratch_shapes = [
        pltpu.VMEM((2, KV_HEADS, BLOCK, GROUP * BLOCK), jnp.float32),
        pltpu.VMEM((SUB // BLOCK, KV_HEADS, 2, GROUP, HALF, LANES), bf16),
        pltpu.VMEM((KV_HEADS, BLOCK + TILE, LANES), bf16),
        pltpu.VMEM((KV_WIDTH, BLOCK + TILE), bf16),
        pltpu.VMEM((SUB, ATTN_WIDTH), jnp.float32),
        pltpu.VMEM((SUB, 3 * CONV_WIDTH), jnp.float32),
        pltpu.VMEM((SUBLANES + TILE, CONV_WIDTH), jnp.float32),
        pltpu.VMEM((SUB, D_MODEL), bf16),
        pltpu.VMEM((SUB, D_MODEL), bf16),
        pltpu.VMEM((SUB, D_MODEL), bf16),
        pltpu.VMEM((SUB, FF_CHUNK), bf16),
    ]
    return pl.pallas_call(
        _layer_kernel,
        grid=(batch, seq // TILE),
        in_specs=in_specs,
        out_specs=tile_spec,
        out_shape=jax.ShapeDtypeStruct(x.shape, x.dtype),
        scratch_shapes=scratch_shapes,
        compiler_params=pltpu.CompilerParams(
            dimension_semantics=("arbitrary", "arbitrary"),
            vmem_limit_bytes=VMEM_LIMIT_BYTES),
        name="hybrid_layer",
    )(x, row(norm1_g[0]), w_in[0].astype(bf16), conv_w[0].astype(jnp.float32),
      sinks[0].astype(jnp.float32), row(attn_norm_g[0]), row(conv_norm_g[0]),
      w_out[0].astype(bf16), row(norm2_g[0]), w_ff1[0].astype(bf16),
      w_ff2[0].astype(bf16), row(final_g))
```

```python
import jax
import jax.numpy as jnp
from jax import lax
from jax.experimental import pallas as pl
from jax.experimental.pallas import tpu as pltpu

D_MODEL = 1024
ATTN_HEADS = 8
KV_HEADS = 2
HEAD_DIM = 64
ATTN_WIDTH = ATTN_HEADS * HEAD_DIM
KV_WIDTH = KV_HEADS * HEAD_DIM
GROUP = ATTN_HEADS // KV_HEADS
WINDOW = 128
BLOCK = 128
HALF = BLOCK // 2
CONV_WIDTH = D_MODEL - ATTN_WIDTH
CONV_K = 3
D_FF = 4 * D_MODEL
EPS = 1e-5
OFF_Q = 0
OFF_KV = ATTN_WIDTH
OFF_C = ATTN_WIDTH + 2 * KV_WIDTH
OFF_B = OFF_C + CONV_WIDTH
OFF_U = OFF_B + CONV_WIDTH
IN_WIDTH = OFF_U + CONV_WIDTH

LANES = 128
SUBLANES = 8
TILE = 1024
SUB = 512
N_SUBS = TILE // SUB
N_CHAINS = 2
CHAIN = SUB // N_CHAINS
CHAIN_BLOCKS = CHAIN // BLOCK
FF_CHUNK = 1024
GATE_TILE = 256
WEIGHT_ROWS = SUB // 2
WEIGHT_COLS = 3 * CONV_WIDTH
MASKED = -1e30
VMEM_LIMIT_BYTES = 60 * 1024 * 1024

assert WINDOW == BLOCK and HEAD_DIM * 2 == LANES and KV_WIDTH == LANES
assert N_SUBS == 2 and N_CHAINS == 2

CONV_PIECE = 256
_STAGES_PER_CHAIN = 19
_EMISSION = "AABBAAAABABABABAABABABABABABABBAABCCBCBDBDBCCCDCDCDCDCCDCDCDCDCDCDCDDCCDDDDD"
assert all(_EMISSION.count(ch) == _STAGES_PER_CHAIN for ch in "ABCD")


def _stage_positions(name):
    return [i for i, ch in enumerate(_EMISSION) if ch == name]


for _first, _second in ("AB", "BC", "CD"):
    for _a, _b in ((1, 2), (6, 6), (7, 7)):
        assert _stage_positions(_first)[_a] < _stage_positions(_second)[_b]
for _first, _second in ("AC", "BD"):
    for _w, _r in ((0, 5), (1, 5), (2, 7), (7, 8), (9, 16), (10, 17)):
        assert _stage_positions(_second)[_w] > _stage_positions(_first)[_r]


def _rms(xf, g):
    inv = lax.rsqrt(jnp.mean(xf * xf, axis=-1, keepdims=True) + EPS)
    return xf * inv * g


def _dot(a, b):
    return jnp.dot(a, b, preferred_element_type=jnp.float32)


def _layer_kernel(x_ref, g1_ref, w_in_hbm, convw_ref, sinks_ref, ga_ref, gc_ref,
                  w_out_hbm, g2_ref, w1_hbm, w2_hbm, gf_ref,
                  o_ref,
                  bias_scr, q_scr, k_scr, vt_scr, attn_scr, gates_scr, cu_scr, mixed_scr,
                  y_scr, z_scr, a_scr, w_in_ref, w_out_ref, w1_ref, w2_ref, load_sem):
    t = pl.program_id(1)
    bf16 = jnp.bfloat16
    f32 = jnp.float32

    col = lax.broadcasted_iota(jnp.int32, (1, GROUP * BLOCK), 1)
    head_col = (col // HALF) % GROUP
    qcol = (col // (GROUP * HALF)) * HALF + col % HALF
    slot_cols = lax.broadcasted_iota(jnp.int32, (BLOCK, GROUP * BLOCK), 0)
    cur_cols = slot_cols <= qcol
    tri = (lax.broadcasted_iota(jnp.int32, (HALF, GROUP * HALF), 0)
           <= lax.broadcasted_iota(jnp.int32, (HALF, GROUP * HALF), 1) % HALF)
    low_half = lax.broadcasted_iota(jnp.int32, (HEAD_DIM, LANES), 1) < HALF

    @pl.when((pl.program_id(0) == 0) & (t == 0))
    def _load_weights():
        pieces = []
        for src, dst in ((w_in_hbm, w_in_ref), (w_out_hbm, w_out_ref), (w1_hbm, w1_ref), (w2_hbm, w2_ref)):
            n_rows, n_cols = dst.shape
            for c0 in range(0, n_cols, WEIGHT_COLS):
                width = min(WEIGHT_COLS, n_cols - c0)
                for r0 in range(0, n_rows, WEIGHT_ROWS):
                    pieces.append((src, dst, r0, c0, width))

        def stage(i):
            return gates_scr.at[(i % 2) * WEIGHT_ROWS:(i % 2 + 1) * WEIGHT_ROWS, 0:pieces[i][4]]

        def copy(i):
            src, _, r0, c0, width = pieces[i]
            return pltpu.make_async_copy(src.at[0, r0:r0 + WEIGHT_ROWS, c0:c0 + width], stage(i),
                                         load_sem.at[i % 2])

        copy(0).start()
        for i, (_, dst, r0, c0, width) in enumerate(pieces):
            if i + 1 < len(pieces):
                copy(i + 1).start()
            copy(i).wait()
            dst[r0:r0 + WEIGHT_ROWS, c0:c0 + width] = stage(i)[...].astype(bf16)

    @pl.when(t == 0)
    def _start_of_sequence():
        k_scr[:, 0:BLOCK, :] = jnp.zeros((KV_HEADS, BLOCK, LANES), bf16)
        vt_scr[:, 0:BLOCK] = jnp.zeros((KV_WIDTH, BLOCK), bf16)
        cu_scr[0:SUBLANES, :] = jnp.zeros((SUBLANES, CONV_WIDTH), f32)
        dist = jnp.where(cur_cols, qcol - slot_cols, qcol - slot_cols + BLOCK).astype(f32)
        for g in range(KV_HEADS):
            slope = jnp.zeros((1, GROUP * BLOCK), f32)
            for hh in range(GROUP):
                slope = jnp.where(head_col == hh, 2.0 ** (-8.0 * (GROUP * g + hh + 1) / ATTN_HEADS), slope)
            b = -slope * dist
            bias_scr[0, g] = b
            bias_scr[1, g] = jnp.where(cur_cols, b, MASKED)

    first = jnp.where(t == 0, 1, 0)
    low = lax.broadcasted_iota(jnp.int32, (CHAIN, LANES), 1) < HEAD_DIM
    zero = jnp.zeros((CHAIN, LANES), f32)
    gate_tiles = list(range(OFF_C, IN_WIDTH, GATE_TILE))

    def chain(sub, c):
        r0 = c * CHAIN
        rows = slice(r0, r0 + CHAIN)
        t0 = sub * SUB + r0
        trows = slice(t0, t0 + CHAIN)
        blocks = range(c * CHAIN_BLOCKS, (c + 1) * CHAIN_BLOCKS)
        tblock = lambda j: sub * (SUB // BLOCK) + j

        x = x_ref[trows, :]
        y_scr[rows, :] = _rms(x, g1_ref[...]).astype(bf16)
        yield

        q = _dot(y_scr[rows, :], w_in_ref[:, OFF_Q:OFF_Q + ATTN_WIDTH]) * (HEAD_DIM ** -0.5)
        for pair in range(ATTN_HEADS // 2):
            qp = q[:, pair * LANES:(pair + 1) * LANES]
            g, hh = divmod(2 * pair, GROUP)
            for r, masked in enumerate((jnp.where(low, qp, zero), jnp.where(low, zero, qp))):
                masked = masked.astype(bf16)
                for jj, j in enumerate(blocks):
                    for half in range(2):
                        q0 = jj * BLOCK + half * HALF
                        q_scr[j, g, half, hh + r] = masked[q0:q0 + HALF]
        kv = _dot(y_scr[rows, :], w_in_ref[:, OFF_KV:OFF_KV + 2 * KV_WIDTH])
        k = kv[:, :KV_WIDTH]
        k_swapped = pltpu.roll(k, HEAD_DIM, axis=1)
        k_scr[0, BLOCK + t0:BLOCK + t0 + CHAIN, :] = jnp.where(low, k, k_swapped).astype(bf16)
        k_scr[1, BLOCK + t0:BLOCK + t0 + CHAIN, :] = jnp.where(low, k_swapped, k).astype(bf16)
        vt_scr[:, BLOCK + t0:BLOCK + t0 + CHAIN] = kv[:, KV_WIDTH:].T.astype(bf16)
        yield

        def gate_piece(i):
            if i < len(gate_tiles):
                c0 = gate_tiles[i]
                gates_scr[rows, c0 - OFF_C:c0 - OFF_C + GATE_TILE] = _dot(
                    y_scr[rows, :], w_in_ref[:, c0:c0 + GATE_TILE])

        def scores(j, g):
            band = k_scr[g, tblock(j) * BLOCK:(tblock(j) + 2) * BLOCK, :]
            nt = (((1,), (1,)), ((), ()))
            q_a = q_scr[j, g, 0].reshape(GROUP * HALF, LANES)
            q_b = q_scr[j, g, 1].reshape(GROUP * HALF, LANES)
            s_a = lax.dot_general(band[:BLOCK + HALF], q_a, nt, preferred_element_type=f32)
            s_b = lax.dot_general(band[HALF:], q_b, nt, preferred_element_type=f32)
            return s_a, s_b

        def softmax(j, g, st):
            s_a, s_b = st
            table = first if tblock(j) == 0 else 0
            logits = jnp.concatenate([
                jnp.concatenate([jnp.where(tri, s_a[BLOCK:], s_a[:HALF]), s_a[HALF:BLOCK]], axis=0),
                jnp.concatenate([s_b[HALF:BLOCK], jnp.where(tri, s_b[BLOCK:], s_b[:HALF])], axis=0),
            ], axis=1) + bias_scr[table, g]
            sink = jnp.zeros((1, GROUP * BLOCK), f32)
            for hh in range(GROUP):
                sink = jnp.where(head_col == hh, sinks_ref[GROUP * g + hh], sink)
            m = jnp.maximum(jnp.max(logits, axis=0, keepdims=True), sink)
            p = jnp.exp(logits - m)
            den = jnp.sum(p, axis=0, keepdims=True) + jnp.exp(sink - m)
            p_a, p_b = p[:, :GROUP * HALF], p[:, GROUP * HALF:]
            zeros = jnp.zeros((HALF, GROUP * HALF), f32)
            pt_a = jnp.concatenate([jnp.where(tri, zeros, p_a[:HALF]), p_a[HALF:],
                                    jnp.where(tri, p_a[:HALF], zeros)], axis=0).astype(bf16)
            pt_b = jnp.concatenate([jnp.where(tri, zeros, p_b[HALF:]), p_b[:HALF],
                                    jnp.where(tri, p_b[HALF:], zeros)], axis=0).astype(bf16)
            return pt_a, pt_b, 1.0 / den

        def values(j, g, pt_a, pt_b, inv):
            vt = vt_scr[g * HEAD_DIM:(g + 1) * HEAD_DIM,
                        tblock(j) * BLOCK:(tblock(j) + 2) * BLOCK]
            o_a = _dot(vt[:, :BLOCK + HALF], pt_a) * inv[:, :GROUP * HALF]
            o_b = _dot(vt[:, HALF:], pt_b) * inv[:, GROUP * HALF:]
            for hp in range(GROUP // 2):
                a0 = o_a[:, hp * LANES:(hp + 1) * LANES]
                a1 = o_b[:, hp * LANES:(hp + 1) * LANES]
                even = jnp.where(low_half, a0, pltpu.roll(a1, HALF, axis=1))
                odd = jnp.where(low_half, pltpu.roll(a0, HALF, axis=1), a1)
                pair = (GROUP // 2) * g + hp
                attn_scr[j * BLOCK:(j + 1) * BLOCK, pair * LANES:(pair + 1) * LANES] = (
                    jnp.concatenate([even, odd], axis=0).T)

        units = [(j, g) for j in blocks for g in range(KV_HEADS)]
        assert len(gate_tiles) <= len(units) + 2
        gate_piece(0)
        st = {0: scores(*units[0]), 1: scores(*units[1])}
        sm = {}
        for i, (j, g) in enumerate(units):
            sm[i] = softmax(j, g, st.pop(i))
            if i + 2 < len(units):
                st[i + 2] = scores(*units[i + 2])
            gate_piece(i + 1)
            if i >= 1:
                values(*units[i - 1], *sm.pop(i - 1))
            if i + 1 == len(units):
                gate_piece(i + 2)
                values(*units[i], *sm.pop(i))
            yield

        conv_pieces = []
        for p0 in range(0, CONV_WIDTH, CONV_PIECE):
            cols = slice(p0, p0 + CONV_PIECE)
            cu = (gates_scr[rows, OFF_C - OFF_C + p0:OFF_C - OFF_C + p0 + CONV_PIECE]
                  * gates_scr[rows, OFF_U - OFF_C + p0:OFF_U - OFF_C + p0 + CONV_PIECE])
            cu_scr[SUBLANES + t0:SUBLANES + t0 + CHAIN, cols] = cu
            cw = convw_ref[:, cols]
            ext = jnp.concatenate([cu_scr[t0:t0 + SUBLANES, cols], cu], axis=0)
            prev1 = pltpu.roll(ext, 1, axis=0)[SUBLANES:]
            prev2 = pltpu.roll(ext, 2, axis=0)[SUBLANES:]
            conv = cw[0:1, :] * prev2 + cw[1:2, :] * prev1 + cw[2:3, :] * cu
            conv_pieces.append(
                gates_scr[rows, OFF_B - OFF_C + p0:OFF_B - OFF_C + p0 + CONV_PIECE] * conv)
            if p0 + CONV_PIECE < CONV_WIDTH:
                yield
        conv = jnp.concatenate(conv_pieces, axis=1)
        mixed_scr[rows, ATTN_WIDTH:] = _rms(conv, gc_ref[...]).astype(bf16)
        mixed_scr[rows, :ATTN_WIDTH] = _rms(attn_scr[rows, :], ga_ref[...]).astype(bf16)
        yield

        o_ref[trows, :] = x + _dot(mixed_scr[rows, :], w_out_ref[...])
        yield

        z_scr[rows, :] = _rms(o_ref[trows, :], g2_ref[...]).astype(bf16)
        yield
        for cc in range(D_FF // FF_CHUNK):
            a = _dot(z_scr[rows, :], w1_ref[:, cc * FF_CHUNK:(cc + 1) * FF_CHUNK])
            a_scr[rows, :] = jnp.square(jnp.maximum(a, 0.0)).astype(bf16)
            yield
            o_ref[trows, :] += _dot(a_scr[rows, :], w2_ref[cc * FF_CHUNK:(cc + 1) * FF_CHUNK, :])
            yield

        o_ref[trows, :] = _rms(o_ref[trows, :], gf_ref[...])

    chains = {"A": chain(0, 0), "B": chain(0, 1), "C": chain(1, 0), "D": chain(1, 1)}
    finished = set()
    for name in _EMISSION:
        assert name not in finished
        if next(chains[name], "done") == "done":
            finished.add(name)
    assert finished == set(chains), "emission plan does not cover every stage"

    k_scr[:, 0:BLOCK, :] = k_scr[:, TILE:TILE + BLOCK, :]
    vt_scr[:, 0:BLOCK] = vt_scr[:, TILE:TILE + BLOCK]
    cu_scr[0:SUBLANES, :] = cu_scr[TILE:TILE + SUBLANES, :]


def _resident(shape):
    return pl.BlockSpec(shape, lambda b, t: (0,) * len(shape),
                        pipeline_mode=pl.Buffered(1))


@jax.jit
def kernel(x, norm1_g, w_in, conv_w, sinks, attn_norm_g, conv_norm_g, w_out, norm2_g,
           w_ff1, w_ff2, final_g):
    batch, seq, d = x.shape
    assert d == D_MODEL and seq % TILE == 0
    assert norm1_g.shape[0] == 1, "single layer"
    bf16 = jnp.bfloat16
    f32 = jnp.float32
    row = lambda v: v.reshape(1, -1).astype(f32)

    tile_spec = pl.BlockSpec((None, TILE, D_MODEL), lambda b, t: (b, t, 0))
    in_hbm = pl.BlockSpec(memory_space=pl.ANY)
    in_specs = [
        tile_spec,
        _resident((1, D_MODEL)),
        in_hbm,
        _resident((CONV_K, CONV_WIDTH)),
        pl.BlockSpec(memory_space=pltpu.SMEM),
        _resident((1, ATTN_WIDTH)),
        _resident((1, CONV_WIDTH)),
        in_hbm,
        _resident((1, D_MODEL)),
        in_hbm,
        in_hbm,
        _resident((1, D_MODEL)),
    ]
    scratch_shapes = [
        pltpu.VMEM((2, KV_HEADS, BLOCK, GROUP * BLOCK), jnp.float32),
        pltpu.VMEM((SUB // BLOCK, KV_HEADS, 2, GROUP, HALF, LANES), bf16),
        pltpu.VMEM((KV_HEADS, BLOCK + TILE, LANES), bf16),
        pltpu.VMEM((KV_WIDTH, BLOCK + TILE), bf16),
        pltpu.VMEM((SUB, ATTN_WIDTH), jnp.float32),
        pltpu.VMEM((SUB, 3 * CONV_WIDTH), jnp.float32),
        pltpu.VMEM((SUBLANES + TILE, CONV_WIDTH), jnp.float32),
        pltpu.VMEM((SUB, D_MODEL), bf16),
        pltpu.VMEM((SUB, D_MODEL), bf16),
        pltpu.VMEM((SUB, D_MODEL), bf16),
        pltpu.VMEM((SUB, FF_CHUNK), bf16),
        pltpu.VMEM((D_MODEL, IN_WIDTH), bf16),
        pltpu.VMEM((D_MODEL, D_MODEL), bf16),
        pltpu.VMEM((D_MODEL, D_FF), bf16),
        pltpu.VMEM((D_FF, D_MODEL), bf16),
        pltpu.SemaphoreType.DMA((2,)),
    ]
    return pl.pallas_call(
        _layer_kernel,
        grid=(batch, seq // TILE),
        in_specs=in_specs,
        out_specs=tile_spec,
        out_shape=jax.ShapeDtypeStruct(x.shape, x.dtype),
        scratch_shapes=scratch_shapes,
        compiler_params=pltpu.CompilerParams(
            dimension_semantics=("arbitrary", "arbitrary"),
            vmem_limit_bytes=VMEM_LIMIT_BYTES),
        name="hybrid_layer",
    )(x, row(norm1_g[0]), w_in.astype(f32), conv_w[0].astype(f32),
      sinks[0].astype(f32), row(attn_norm_g[0]), row(conv_norm_g[0]),
      w_out.astype(f32), row(norm2_g[0]), w_ff1.astype(f32),
      w_ff2.astype(f32), row(final_g))
```

```python
import jax
import jax.numpy as jnp
from jax import lax
from jax.experimental import pallas as pl
from jax.experimental.pallas import tpu as pltpu

D_MODEL = 1024
ATTN_HEADS = 8
KV_HEADS = 2
HEAD_DIM = 64
ATTN_WIDTH = ATTN_HEADS * HEAD_DIM
KV_WIDTH = KV_HEADS * HEAD_DIM
GROUP = ATTN_HEADS // KV_HEADS
WINDOW = 128
BLOCK = 128
HALF = BLOCK // 2
CONV_WIDTH = D_MODEL - ATTN_WIDTH
CONV_K = 3
D_FF = 4 * D_MODEL
EPS = 1e-5
OFF_Q = 0
OFF_KV = ATTN_WIDTH
OFF_C = ATTN_WIDTH + 2 * KV_WIDTH
OFF_B = OFF_C + CONV_WIDTH
OFF_U = OFF_B + CONV_WIDTH
IN_WIDTH = OFF_U + CONV_WIDTH

LANES = 128
SUBLANES = 8
TILE = 1024
SUB = 512
N_SUBS = TILE // SUB
N_CHAINS = 2
CHAIN = SUB // N_CHAINS
CHAIN_BLOCKS = CHAIN // BLOCK
FF_CHUNK = 1024
GATE_TILE = 256
MASKED = -1e30
VMEM_LIMIT_BYTES = 60 * 1024 * 1024

assert WINDOW == BLOCK and HEAD_DIM * 2 == LANES and KV_WIDTH == LANES
assert N_SUBS == 2 and N_CHAINS == 2

CONV_PIECE = 256
_STAGES_PER_CHAIN = 19
_EMISSION = "AABBAAAABABABABAABABABABABABABBAABCCBCBDBDBCCDCCDCDCDCCDCDCDCDCDCDCDDCCDDDDD"
assert all(_EMISSION.count(ch) == _STAGES_PER_CHAIN for ch in "ABCD")


def _stage_positions(name):
    return [i for i, ch in enumerate(_EMISSION) if ch == name]


for _first, _second in ("AB", "BC", "CD"):
    for _a, _b in ((1, 2), (6, 6), (7, 7)):
        assert _stage_positions(_first)[_a] < _stage_positions(_second)[_b]
for _first, _second in ("AC", "BD"):
    for _w, _r in ((0, 5), (1, 5), (2, 7), (7, 8), (9, 16), (10, 17)):
        assert _stage_positions(_second)[_w] > _stage_positions(_first)[_r]


def _rms(xf, g):
    inv = lax.rsqrt(jnp.mean(xf * xf, axis=-1, keepdims=True) + EPS)
    return xf * inv * g


def _dot(a, b):
    return jnp.dot(a, b, preferred_element_type=jnp.float32)


def _layer_kernel(x_ref, g1_ref, w_in_ref, convw_ref, sinks_ref, ga_ref, gc_ref,
                  w_out_ref, g2_ref, w1_ref, w2_ref, gf_ref,
                  o_ref,
                  bias_scr, q_scr, k_scr, vt_scr, attn_scr, gates_scr, cu_scr, mixed_scr,
                  y_scr, z_scr, a_scr):
    t = pl.program_id(1)
    bf16 = jnp.bfloat16
    f32 = jnp.float32

    col = lax.broadcasted_iota(jnp.int32, (1, GROUP * BLOCK), 1)
    head_col = (col // HALF) % GROUP
    qcol = (col // (GROUP * HALF)) * HALF + col % HALF
    slot_cols = lax.broadcasted_iota(jnp.int32, (BLOCK, GROUP * BLOCK), 0)
    cur_cols = slot_cols <= qcol
    tri = (lax.broadcasted_iota(jnp.int32, (HALF, GROUP * HALF), 0)
           <= lax.broadcasted_iota(jnp.int32, (HALF, GROUP * HALF), 1) % HALF)
    low_half = lax.broadcasted_iota(jnp.int32, (HEAD_DIM, LANES), 1) < HALF

    @pl.when(t == 0)
    def _start_of_sequence():
        k_scr[:, 0:BLOCK, :] = jnp.zeros((KV_HEADS, BLOCK, LANES), bf16)
        vt_scr[:, 0:BLOCK] = jnp.zeros((KV_WIDTH, BLOCK), bf16)
        cu_scr[0:SUBLANES, :] = jnp.zeros((SUBLANES, CONV_WIDTH), f32)
        dist = jnp.where(cur_cols, qcol - slot_cols, qcol - slot_cols + BLOCK).astype(f32)
        for g in range(KV_HEADS):
            slope = jnp.zeros((1, GROUP * BLOCK), f32)
            for hh in range(GROUP):
                slope = jnp.where(head_col == hh, 2.0 ** (-8.0 * (GROUP * g + hh + 1) / ATTN_HEADS), slope)
            b = -slope * dist
            bias_scr[0, g] = b
            bias_scr[1, g] = jnp.where(cur_cols, b, MASKED)

    first = jnp.where(t == 0, 1, 0)
    low = lax.broadcasted_iota(jnp.int32, (CHAIN, LANES), 1) < HEAD_DIM
    zero = jnp.zeros((CHAIN, LANES), f32)
    gate_tiles = list(range(OFF_C, IN_WIDTH, GATE_TILE))

    def chain(sub, c):
        r0 = c * CHAIN
        rows = slice(r0, r0 + CHAIN)
        t0 = sub * SUB + r0
        trows = slice(t0, t0 + CHAIN)
        blocks = range(c * CHAIN_BLOCKS, (c + 1) * CHAIN_BLOCKS)
        tblock = lambda j: sub * (SUB // BLOCK) + j

        x = x_ref[trows, :]
        y_scr[rows, :] = _rms(x, g1_ref[...]).astype(bf16)
        yield

        q = _dot(y_scr[rows, :], w_in_ref[:, OFF_Q:OFF_Q + ATTN_WIDTH]) * (HEAD_DIM ** -0.5)
        for pair in range(ATTN_HEADS // 2):
            qp = q[:, pair * LANES:(pair + 1) * LANES]
            g, hh = divmod(2 * pair, GROUP)
            for r, masked in enumerate((jnp.where(low, qp, zero), jnp.where(low, zero, qp))):
                masked = masked.astype(bf16)
                for jj, j in enumerate(blocks):
                    for half in range(2):
                        q0 = jj * BLOCK + half * HALF
                        q_scr[j, g, half, hh + r] = masked[q0:q0 + HALF]
        kv = _dot(y_scr[rows, :], w_in_ref[:, OFF_KV:OFF_KV + 2 * KV_WIDTH])
        k = kv[:, :KV_WIDTH]
        k_swapped = pltpu.roll(k, HEAD_DIM, axis=1)
        k_scr[0, BLOCK + t0:BLOCK + t0 + CHAIN, :] = jnp.where(low, k, k_swapped).astype(bf16)
        k_scr[1, BLOCK + t0:BLOCK + t0 + CHAIN, :] = jnp.where(low, k_swapped, k).astype(bf16)
        vt_scr[:, BLOCK + t0:BLOCK + t0 + CHAIN] = kv[:, KV_WIDTH:].T.astype(bf16)
        yield

        def gate_piece(i):
            if i < len(gate_tiles):
                c0 = gate_tiles[i]
                gates_scr[rows, c0 - OFF_C:c0 - OFF_C + GATE_TILE] = _dot(
                    y_scr[rows, :], w_in_ref[:, c0:c0 + GATE_TILE])

        def scores(j, g):
            band = k_scr[g, tblock(j) * BLOCK:(tblock(j) + 2) * BLOCK, :]
            nt = (((1,), (1,)), ((), ()))
            q_a = q_scr[j, g, 0].reshape(GROUP * HALF, LANES)
            q_b = q_scr[j, g, 1].reshape(GROUP * HALF, LANES)
            s_a = lax.dot_general(band[:BLOCK + HALF], q_a, nt, preferred_element_type=f32)
            s_b = lax.dot_general(band[HALF:], q_b, nt, preferred_element_type=f32)
            return s_a, s_b

        def softmax(j, g, st):
            s_a, s_b = st
            table = first if tblock(j) == 0 else 0
            logits = jnp.concatenate([
                jnp.concatenate([jnp.where(tri, s_a[BLOCK:], s_a[:HALF]), s_a[HALF:BLOCK]], axis=0),
                jnp.concatenate([s_b[HALF:BLOCK], jnp.where(tri, s_b[BLOCK:], s_b[:HALF])], axis=0),
            ], axis=1) + bias_scr[table, g]
            sink = jnp.zeros((1, GROUP * BLOCK), f32)
            for hh in range(GROUP):
                sink = jnp.where(head_col == hh, sinks_ref[GROUP * g + hh], sink)
            m = jnp.maximum(jnp.max(logits, axis=0, keepdims=True), sink)
            p = jnp.exp(logits - m)
            den = jnp.sum(p, axis=0, keepdims=True) + jnp.exp(sink - m)
            p_a, p_b = p[:, :GROUP * HALF], p[:, GROUP * HALF:]
            zeros = jnp.zeros((HALF, GROUP * HALF), f32)
            pt_a = jnp.concatenate([jnp.where(tri, zeros, p_a[:HALF]), p_a[HALF:],
                                    jnp.where(tri, p_a[:HALF], zeros)], axis=0).astype(bf16)
            pt_b = jnp.concatenate([jnp.where(tri, zeros, p_b[HALF:]), p_b[:HALF],
                                    jnp.where(tri, p_b[HALF:], zeros)], axis=0).astype(bf16)
            return pt_a, pt_b, 1.0 / den

        def values(j, g, pt_a, pt_b, inv):
            vt = vt_scr[g * HEAD_DIM:(g + 1) * HEAD_DIM,
                        tblock(j) * BLOCK:(tblock(j) + 2) * BLOCK]
            o_a = _dot(vt[:, :BLOCK + HALF], pt_a) * inv[:, :GROUP * HALF]
            o_b = _dot(vt[:, HALF:], pt_b) * inv[:, GROUP * HALF:]
            for hp in range(GROUP // 2):
                a0 = o_a[:, hp * LANES:(hp + 1) * LANES]
                a1 = o_b[:, hp * LANES:(hp + 1) * LANES]
                even = jnp.where(low_half, a0, pltpu.roll(a1, HALF, axis=1))
                odd = jnp.where(low_half, pltpu.roll(a0, HALF, axis=1), a1)
                pair = (GROUP // 2) * g + hp
                attn_scr[j * BLOCK:(j + 1) * BLOCK, pair * LANES:(pair + 1) * LANES] = (
                    jnp.concatenate([even, odd], axis=0).T)

        units = [(j, g) for j in blocks for g in range(KV_HEADS)]
        assert len(gate_tiles) <= len(units) + 2
        gate_piece(0)
        st = {0: scores(*units[0]), 1: scores(*units[1])}
        sm = {}
        for i, (j, g) in enumerate(units):
            sm[i] = softmax(j, g, st.pop(i))
            if i + 2 < len(units):
                st[i + 2] = scores(*units[i + 2])
            gate_piece(i + 1)
            if i >= 1:
                values(*units[i - 1], *sm.pop(i - 1))
            if i + 1 == len(units):
                gate_piece(i + 2)
                values(*units[i], *sm.pop(i))
            yield

        conv_pieces = []
        for p0 in range(0, CONV_WIDTH, CONV_PIECE):
            cols = slice(p0, p0 + CONV_PIECE)
            cu = (gates_scr[rows, OFF_C - OFF_C + p0:OFF_C - OFF_C + p0 + CONV_PIECE]
                  * gates_scr[rows, OFF_U - OFF_C + p0:OFF_U - OFF_C + p0 + CONV_PIECE])
            cu_scr[SUBLANES + t0:SUBLANES + t0 + CHAIN, cols] = cu
            cw = convw_ref[:, cols]
            ext = jnp.concatenate([cu_scr[t0:t0 + SUBLANES, cols], cu], axis=0)
            prev1 = pltpu.roll(ext, 1, axis=0)[SUBLANES:]
            prev2 = pltpu.roll(ext, 2, axis=0)[SUBLANES:]
            conv = cw[0:1, :] * prev2 + cw[1:2, :] * prev1 + cw[2:3, :] * cu
            conv_pieces.append(
                gates_scr[rows, OFF_B - OFF_C + p0:OFF_B - OFF_C + p0 + CONV_PIECE] * conv)
            if p0 + CONV_PIECE < CONV_WIDTH:
                yield
        conv = jnp.concatenate(conv_pieces, axis=1)
        mixed_scr[rows, ATTN_WIDTH:] = _rms(conv, gc_ref[...]).astype(bf16)
        mixed_scr[rows, :ATTN_WIDTH] = _rms(attn_scr[rows, :], ga_ref[...]).astype(bf16)
        yield

        o_ref[trows, :] = x + _dot(mixed_scr[rows, :], w_out_ref[...])
        yield

        z_scr[rows, :] = _rms(o_ref[trows, :], g2_ref[...]).astype(bf16)
        yield
        for cc in range(D_FF // FF_CHUNK):
            a = _dot(z_scr[rows, :], w1_ref[:, cc * FF_CHUNK:(cc + 1) * FF_CHUNK])
            a_scr[rows, :] = jnp.square(jnp.maximum(a, 0.0)).astype(bf16)
            yield
            o_ref[trows, :] += _dot(a_scr[rows, :], w2_ref[cc * FF_CHUNK:(cc + 1) * FF_CHUNK, :])
            yield

        o_ref[trows, :] = _rms(o_ref[trows, :], gf_ref[...])

    chains = {"A": chain(0, 0), "B": chain(0, 1), "C": chain(1, 0), "D": chain(1, 1)}
    finished = set()
    for name in _EMISSION:
        assert name not in finished
        if next(chains[name], "done") == "done":
            finished.add(name)
    assert finished == set(chains), "emission plan does not cover every stage"

    k_scr[:, 0:BLOCK, :] = k_scr[:, TILE:TILE + BLOCK, :]
    vt_scr[:, 0:BLOCK] = vt_scr[:, TILE:TILE + BLOCK]
    cu_scr[0:SUBLANES, :] = cu_scr[TILE:TILE + SUBLANES, :]


def _resident(shape):
    return pl.BlockSpec(shape, lambda b, t: (0,) * len(shape),
                        pipeline_mode=pl.Buffered(1))


@jax.jit
def kernel(x, norm1_g, w_in, conv_w, sinks, attn_norm_g, conv_norm_g, w_out, norm2_g,
           w_ff1, w_ff2, final_g):
    batch, seq, d = x.shape
    assert d == D_MODEL and seq % TILE == 0
    assert norm1_g.shape[0] == 1, "single layer"
    bf16 = jnp.bfloat16
    row = lambda v: v.reshape(1, -1).astype(jnp.float32)

    tile_spec = pl.BlockSpec((None, TILE, D_MODEL), lambda b, t: (b, t, 0))
    in_specs = [
        tile_spec,
        _resident((1, D_MODEL)),
        _resident((D_MODEL, IN_WIDTH)),
        _resident((CONV_K, CONV_WIDTH)),
        pl.BlockSpec(memory_space=pltpu.SMEM),
        _resident((1, ATTN_WIDTH)),
        _resident((1, CONV_WIDTH)),
        _resident((D_MODEL, D_MODEL)),
        _resident((1, D_MODEL)),
        _resident((D_MODEL, D_FF)),
        _resident((D_FF, D_MODEL)),
        _resident((1, D_MODEL)),
    ]
    scratch_shapes = [
        pltpu.VMEM((2, KV_HEADS, BLOCK, GROUP * BLOCK), jnp.float32),
        pltpu.VMEM((SUB // BLOCK, KV_HEADS, 2, GROUP, HALF, LANES), bf16),
        pltpu.VMEM((KV_HEADS, BLOCK + TILE, LANES), bf16),
        pltpu.VMEM((KV_WIDTH, BLOCK + TILE), bf16),
        pltpu.VMEM((SUB, ATTN_WIDTH), jnp.float32),
        pltpu.VMEM((SUB, 3 * CONV_WIDTH), jnp.float32),
        pltpu.VMEM((SUBLANES + TILE, CONV_WIDTH), jnp.float32),
        pltpu.VMEM((SUB, D_MODEL), bf16),
        pltpu.VMEM((SUB, D_MODEL), bf16),
        pltpu.VMEM((SUB, D_MODEL), bf16),
        pltpu.VMEM((SUB, FF_CHUNK), bf16),
    ]
    return pl.pallas_call(
        _layer_kernel,
        grid=(batch, seq // TILE),
        in_specs=in_specs,
        out_specs=tile_spec,
        out_shape=jax.ShapeDtypeStruct(x.shape, x.dtype),
        scratch_shapes=scratch_shapes,
        compiler_params=pltpu.CompilerParams(
            dimension_semantics=("arbitrary", "arbitrary"),
            vmem_limit_bytes=VMEM_LIMIT_BYTES),
        name="hybrid_layer",
    )(x, row(norm1_g[0]), w_in[0].astype(bf16), conv_w[0].astype(jnp.float32),
      sinks[0].astype(jnp.float32), row(attn_norm_g[0]), row(conv_norm_g[0]),
      w_out[0].astype(bf16), row(norm2_g[0]), w_ff1[0].astype(bf16),
      w_ff2[0].astype(bf16), row(final_g))
```

```python
import jax
import jax.numpy as jnp
from jax import lax
from jax.experimental import pallas as pl
from jax.experimental.pallas import tpu as pltpu

D_MODEL = 1024
ATTN_HEADS = 8
KV_HEADS = 2
HEAD_DIM = 64
ATTN_WIDTH = ATTN_HEADS * HEAD_DIM
KV_WIDTH = KV_HEADS * HEAD_DIM
GROUP = ATTN_HEADS // KV_HEADS
WINDOW = 128
BLOCK = 128
HALF = BLOCK // 2
CONV_WIDTH = D_MODEL - ATTN_WIDTH
CONV_K = 3
D_FF = 4 * D_MODEL
EPS = 1e-5
OFF_Q = 0
OFF_KV = ATTN_WIDTH
OFF_C = ATTN_WIDTH + 2 * KV_WIDTH
OFF_B = OFF_C + CONV_WIDTH
OFF_U = OFF_B + CONV_WIDTH
IN_WIDTH = OFF_U + CONV_WIDTH

LANES = 128
SUBLANES = 8
TILE = 512
N_CHAINS = 2
CHAIN = TILE // N_CHAINS
CHAIN_BLOCKS = CHAIN // BLOCK
FF_CHUNK = 1024
GATE_TILE = 256
MASKED = -1e30
VMEM_LIMIT_BYTES = 56 * 1024 * 1024

assert WINDOW == BLOCK and HEAD_DIM * 2 == LANES and KV_WIDTH == LANES

CONV_PIECE = 256
_STAGES_PER_CHAIN = 19
_EMISSION = "AABBAAAABABABABAABABABABABABABBAABBBBB"
assert _EMISSION.count("A") == _EMISSION.count("B") == _STAGES_PER_CHAIN


def _stage_positions(name):
    return [i for i, ch in enumerate(_EMISSION) if ch == name]


for _a, _b in ((1, 2), (6, 6), (7, 7)):
    assert _stage_positions("A")[_a] < _stage_positions("B")[_b]


def _rms(xf, g):
    inv = lax.rsqrt(jnp.mean(xf * xf, axis=-1, keepdims=True) + EPS)
    return xf * inv * g


def _dot(a, b):
    return jnp.dot(a, b, preferred_element_type=jnp.float32)


def _layer_kernel(x_ref, g1_ref, w_in_ref, convw_ref, sinks_ref, ga_ref, gc_ref,
                  w_out_ref, g2_ref, w1_ref, w2_ref, gf_ref,
                  o_ref,
                  bias_scr, q_scr, k_scr, vt_scr, attn_scr, gates_scr, cu_scr, mixed_scr,
                  y_scr, z_scr, a_scr):
    t = pl.program_id(1)
    bf16 = jnp.bfloat16
    f32 = jnp.float32

    col = lax.broadcasted_iota(jnp.int32, (1, GROUP * BLOCK), 1)
    head_col = (col // HALF) % GROUP
    qcol = (col // (GROUP * HALF)) * HALF + col % HALF
    slot_cols = lax.broadcasted_iota(jnp.int32, (BLOCK, GROUP * BLOCK), 0)
    cur_cols = slot_cols <= qcol
    tri = (lax.broadcasted_iota(jnp.int32, (HALF, GROUP * HALF), 0)
           <= lax.broadcasted_iota(jnp.int32, (HALF, GROUP * HALF), 1) % HALF)
    low_half = lax.broadcasted_iota(jnp.int32, (HEAD_DIM, LANES), 1) < HALF

    @pl.when(t == 0)
    def _start_of_sequence():
        k_scr[:, 0:BLOCK, :] = jnp.zeros((KV_HEADS, BLOCK, LANES), bf16)
        vt_scr[:, 0:BLOCK] = jnp.zeros((KV_WIDTH, BLOCK), bf16)
        cu_scr[0:SUBLANES, :] = jnp.zeros((SUBLANES, CONV_WIDTH), f32)
        dist = jnp.where(cur_cols, qcol - slot_cols, qcol - slot_cols + BLOCK).astype(f32)
        for g in range(KV_HEADS):
            slope = jnp.zeros((1, GROUP * BLOCK), f32)
            for hh in range(GROUP):
                slope = jnp.where(head_col == hh, 2.0 ** (-8.0 * (GROUP * g + hh + 1) / ATTN_HEADS), slope)
            b = -slope * dist
            bias_scr[0, g] = b
            bias_scr[1, g] = jnp.where(cur_cols, b, MASKED)

    first = jnp.where(t == 0, 1, 0)
    low = lax.broadcasted_iota(jnp.int32, (CHAIN, LANES), 1) < HEAD_DIM
    zero = jnp.zeros((CHAIN, LANES), f32)
    gate_tiles = list(range(OFF_C, IN_WIDTH, GATE_TILE))

    def chain(c):
        r0 = c * CHAIN
        rows = slice(r0, r0 + CHAIN)
        blocks = range(c * CHAIN_BLOCKS, (c + 1) * CHAIN_BLOCKS)

        x = x_ref[rows, :]
        y_scr[rows, :] = _rms(x, g1_ref[...]).astype(bf16)
        yield

        q = _dot(y_scr[rows, :], w_in_ref[:, OFF_Q:OFF_Q + ATTN_WIDTH]) * (HEAD_DIM ** -0.5)
        for pair in range(ATTN_HEADS // 2):
            qp = q[:, pair * LANES:(pair + 1) * LANES]
            g, hh = divmod(2 * pair, GROUP)
            for r, masked in enumerate((jnp.where(low, qp, zero), jnp.where(low, zero, qp))):
                masked = masked.astype(bf16)
                for jj, j in enumerate(blocks):
                    for half in range(2):
                        q0 = jj * BLOCK + half * HALF
                        q_scr[j, g, half, hh + r] = masked[q0:q0 + HALF]
        kv = _dot(y_scr[rows, :], w_in_ref[:, OFF_KV:OFF_KV + 2 * KV_WIDTH])
        k = kv[:, :KV_WIDTH]
        k_swapped = pltpu.roll(k, HEAD_DIM, axis=1)
        k_scr[0, BLOCK + r0:BLOCK + r0 + CHAIN, :] = jnp.where(low, k, k_swapped).astype(bf16)
        k_scr[1, BLOCK + r0:BLOCK + r0 + CHAIN, :] = jnp.where(low, k_swapped, k).astype(bf16)
        vt_scr[:, BLOCK + r0:BLOCK + r0 + CHAIN] = kv[:, KV_WIDTH:].T.astype(bf16)
        yield

        def gate_piece(i):
            if i < len(gate_tiles):
                c0 = gate_tiles[i]
                gates_scr[rows, c0 - OFF_C:c0 - OFF_C + GATE_TILE] = _dot(
                    y_scr[rows, :], w_in_ref[:, c0:c0 + GATE_TILE])

        def scores(j, g):
            band = k_scr[g, j * BLOCK:(j + 2) * BLOCK, :]
            nt = (((1,), (1,)), ((), ()))
            q_a = q_scr[j, g, 0].reshape(GROUP * HALF, LANES)
            q_b = q_scr[j, g, 1].reshape(GROUP * HALF, LANES)
            s_a = lax.dot_general(band[:BLOCK + HALF], q_a, nt, preferred_element_type=f32)
            s_b = lax.dot_general(band[HALF:], q_b, nt, preferred_element_type=f32)
            return s_a, s_b

        def softmax(j, g, st):
            s_a, s_b = st
            table = first if j == 0 else 0
            logits = jnp.concatenate([
                jnp.concatenate([jnp.where(tri, s_a[BLOCK:], s_a[:HALF]), s_a[HALF:BLOCK]], axis=0),
                jnp.concatenate([s_b[HALF:BLOCK], jnp.where(tri, s_b[BLOCK:], s_b[:HALF])], axis=0),
            ], axis=1) + bias_scr[table, g]
            sink = jnp.zeros((1, GROUP * BLOCK), f32)
            for hh in range(GROUP):
                sink = jnp.where(head_col == hh, sinks_ref[GROUP * g + hh], sink)
            m = jnp.maximum(jnp.max(logits, axis=0, keepdims=True), sink)
            p = jnp.exp(logits - m)
            den = jnp.sum(p, axis=0, keepdims=True) + jnp.exp(sink - m)
            p_a, p_b = p[:, :GROUP * HALF], p[:, GROUP * HALF:]
            zeros = jnp.zeros((HALF, GROUP * HALF), f32)
            pt_a = jnp.concatenate([jnp.where(tri, zeros, p_a[:HALF]), p_a[HALF:],
                                    jnp.where(tri, p_a[:HALF], zeros)], axis=0).astype(bf16)
            pt_b = jnp.concatenate([jnp.where(tri, zeros, p_b[HALF:]), p_b[:HALF],
                                    jnp.where(tri, p_b[HALF:], zeros)], axis=0).astype(bf16)
            return pt_a, pt_b, 1.0 / den

        def values(j, g, pt_a, pt_b, inv):
            vt = vt_scr[g * HEAD_DIM:(g + 1) * HEAD_DIM, j * BLOCK:(j + 2) * BLOCK]
            o_a = _dot(vt[:, :BLOCK + HALF], pt_a) * inv[:, :GROUP * HALF]
            o_b = _dot(vt[:, HALF:], pt_b) * inv[:, GROUP * HALF:]
            for hp in range(GROUP // 2):
                a0 = o_a[:, hp * LANES:(hp + 1) * LANES]
                a1 = o_b[:, hp * LANES:(hp + 1) * LANES]
                even = jnp.where(low_half, a0, pltpu.roll(a1, HALF, axis=1))
                odd = jnp.where(low_half, pltpu.roll(a0, HALF, axis=1), a1)
                pair = (GROUP // 2) * g + hp
                attn_scr[j * BLOCK:(j + 1) * BLOCK, pair * LANES:(pair + 1) * LANES] = (
                    jnp.concatenate([even, odd], axis=0).T)

        units = [(j, g) for j in blocks for g in range(KV_HEADS)]
        assert len(gate_tiles) <= len(units) + 2
        gate_piece(0)
        st = {0: scores(*units[0]), 1: scores(*units[1])}
        sm = {}
        for i, (j, g) in enumerate(units):
            sm[i] = softmax(j, g, st.pop(i))
            if i + 2 < len(units):
                st[i + 2] = scores(*units[i + 2])
            gate_piece(i + 1)
            if i >= 1:
                values(*units[i - 1], *sm.pop(i - 1))
            if i + 1 == len(units):
                gate_piece(i + 2)
                values(*units[i], *sm.pop(i))
            yield

        conv_pieces = []
        for p0 in range(0, CONV_WIDTH, CONV_PIECE):
            cols = slice(p0, p0 + CONV_PIECE)
            cu = (gates_scr[rows, OFF_C - OFF_C + p0:OFF_C - OFF_C + p0 + CONV_PIECE]
                  * gates_scr[rows, OFF_U - OFF_C + p0:OFF_U - OFF_C + p0 + CONV_PIECE])
            cu_scr[SUBLANES + r0:SUBLANES + r0 + CHAIN, cols] = cu
            cw = convw_ref[:, cols]
            ext = jnp.concatenate([cu_scr[r0:r0 + SUBLANES, cols], cu], axis=0)
            prev1 = pltpu.roll(ext, 1, axis=0)[SUBLANES:]
            prev2 = pltpu.roll(ext, 2, axis=0)[SUBLANES:]
            conv = cw[0:1, :] * prev2 + cw[1:2, :] * prev1 + cw[2:3, :] * cu
            conv_pieces.append(
                gates_scr[rows, OFF_B - OFF_C + p0:OFF_B - OFF_C + p0 + CONV_PIECE] * conv)
            if p0 + CONV_PIECE < CONV_WIDTH:
                yield
        conv = jnp.concatenate(conv_pieces, axis=1)
        mixed_scr[rows, ATTN_WIDTH:] = _rms(conv, gc_ref[...]).astype(bf16)
        mixed_scr[rows, :ATTN_WIDTH] = _rms(attn_scr[rows, :], ga_ref[...]).astype(bf16)
        yield

        o_ref[rows, :] = x + _dot(mixed_scr[rows, :], w_out_ref[...])
        yield

        z_scr[rows, :] = _rms(o_ref[rows, :], g2_ref[...]).astype(bf16)
        yield
        for cc in range(D_FF // FF_CHUNK):
            a = _dot(z_scr[rows, :], w1_ref[:, cc * FF_CHUNK:(cc + 1) * FF_CHUNK])
            a_scr[rows, :] = jnp.square(jnp.maximum(a, 0.0)).astype(bf16)
            yield
            o_ref[rows, :] += _dot(a_scr[rows, :], w2_ref[cc * FF_CHUNK:(cc + 1) * FF_CHUNK, :])
            yield

        o_ref[rows, :] = _rms(o_ref[rows, :], gf_ref[...])

    chains = {"A": chain(0), "B": chain(1)}
    finished = set()
    for name in _EMISSION:
        assert name not in finished
        if next(chains[name], "done") == "done":
            finished.add(name)
    assert finished == set(chains), "emission plan does not cover every stage"

    k_scr[:, 0:BLOCK, :] = k_scr[:, TILE:TILE + BLOCK, :]
    vt_scr[:, 0:BLOCK] = vt_scr[:, TILE:TILE + BLOCK]
    cu_scr[0:SUBLANES, :] = cu_scr[TILE:TILE + SUBLANES, :]


def _resident(shape):
    return pl.BlockSpec(shape, lambda b, t: (0,) * len(shape),
                        pipeline_mode=pl.Buffered(1))


@jax.jit
def kernel(x, norm1_g, w_in, conv_w, sinks, attn_norm_g, conv_norm_g, w_out, norm2_g,
           w_ff1, w_ff2, final_g):
    batch, seq, d = x.shape
    assert d == D_MODEL and seq % TILE == 0
    assert norm1_g.shape[0] == 1, "single layer"
    bf16 = jnp.bfloat16
    row = lambda v: v.reshape(1, -1).astype(jnp.float32)

    tile_spec = pl.BlockSpec((None, TILE, D_MODEL), lambda b, t: (b, t, 0))
    in_specs = [
        tile_spec,
        _resident((1, D_MODEL)),
        _resident((D_MODEL, IN_WIDTH)),
        _resident((CONV_K, CONV_WIDTH)),
        pl.BlockSpec(memory_space=pltpu.SMEM),
        _resident((1, ATTN_WIDTH)),
        _resident((1, CONV_WIDTH)),
        _resident((D_MODEL, D_MODEL)),
        _resident((1, D_MODEL)),
        _resident((D_MODEL, D_FF)),
        _resident((D_FF, D_MODEL)),
        _resident((1, D_MODEL)),
    ]
    scratch_shapes = [
        pltpu.VMEM((2, KV_HEADS, BLOCK, GROUP * BLOCK), jnp.float32),
        pltpu.VMEM((TILE // BLOCK, KV_HEADS, 2, GROUP, HALF, LANES), bf16),
        pltpu.VMEM((KV_HEADS, BLOCK + TILE, LANES), bf16),
        pltpu.VMEM((KV_WIDTH, BLOCK + TILE), bf16),
        pltpu.VMEM((TILE,ATTN_WIDTH), jnp.float32),
        pltpu.VMEM((TILE,3 * CONV_WIDTH), jnp.float32),
        pltpu.VMEM((SUBLANES + TILE, CONV_WIDTH), jnp.float32),
        pltpu.VMEM((TILE,D_MODEL), bf16),
        pltpu.VMEM((TILE,D_MODEL), bf16),
        pltpu.VMEM((TILE,D_MODEL), bf16),
        pltpu.VMEM((TILE,FF_CHUNK), bf16),
    ]
    return pl.pallas_call(
        _layer_kernel,
        grid=(batch, seq // TILE),
        in_specs=in_specs,
        out_specs=tile_spec,
        out_shape=jax.ShapeDtypeStruct(x.shape, x.dtype),
        scratch_shapes=scratch_shapes,
        compiler_params=pltpu.CompilerParams(
            dimension_semantics=("arbitrary", "arbitrary"),
            vmem_limit_bytes=VMEM_LIMIT_BYTES),
        name="hybrid_layer",
    )(x, row(norm1_g[0]), w_in[0].astype(bf16), conv_w[0].astype(jnp.float32),
      sinks[0].astype(jnp.float32), row(attn_norm_g[0]), row(conv_norm_g[0]),
      w_out[0].astype(bf16), row(norm2_g[0]), w_ff1[0].astype(bf16),
      w_ff2[0].astype(bf16), row(final_g))
```

```python
import functools

import jax
import jax.numpy as jnp
from jax import lax
from jax.experimental import pallas as pl
from jax.experimental.pallas import tpu as pltpu

D_MODEL = 1024
ATTN_HEADS = 8
KV_HEADS = 2
HEAD_DIM = 64
ATTN_WIDTH = ATTN_HEADS * HEAD_DIM
KV_WIDTH = KV_HEADS * HEAD_DIM
GROUP = ATTN_HEADS // KV_HEADS
WINDOW = 128
BLOCK = 128
HALF = BLOCK // 2
CONV_WIDTH = D_MODEL - ATTN_WIDTH
CONV_K = 3
D_FF = 4 * D_MODEL
EPS = 1e-5
OFF_Q = 0
OFF_KV = ATTN_WIDTH
OFF_C = ATTN_WIDTH + 2 * KV_WIDTH
OFF_B = OFF_C + CONV_WIDTH
OFF_U = OFF_B + CONV_WIDTH
IN_WIDTH = OFF_U + CONV_WIDTH

LANES = 128
SUBLANES = 8
TILE = 512
N_CHAINS = 2
CHAIN = TILE // N_CHAINS
CHAIN_BLOCKS = CHAIN // BLOCK
FF_CHUNK = 1024
GATE_TILE = 256
MASKED = -1e30
VMEM_LIMIT_BYTES = 56 * 1024 * 1024

assert WINDOW == BLOCK and HEAD_DIM * 2 == LANES and KV_WIDTH == LANES

CONV_PIECE = 256
_STAGES_PER_CHAIN = 19
_FRONT_STAGES = 8
_EMISSION = "CACACBCCBDACADACDACDBCADBDCACBDBDDBDBD"
assert _EMISSION.count("A") == _EMISSION.count("B") == _FRONT_STAGES
assert _EMISSION.count("C") == _EMISSION.count("D") == _STAGES_PER_CHAIN - _FRONT_STAGES


def _stage_positions(name):
    return [i for i, ch in enumerate(_EMISSION) if ch == name]


for _a, _b in ((1, 2), (6, 6), (7, 7)):
    assert _stage_positions("A")[_a] < _stage_positions("B")[_b]
for _back, _front in ("CA", "DB"):
    assert _stage_positions(_back)[0] < _stage_positions(_front)[_FRONT_STAGES - 1]


def _rms(xf, g):
    inv = lax.rsqrt(jnp.mean(xf * xf, axis=-1, keepdims=True) + EPS)
    return xf * inv * g


def _dot(a, b):
    return jnp.dot(a, b, preferred_element_type=jnp.float32)


def _layer_kernel(n_tiles, total, x_ref, xp_ref, g1_ref, w_in_ref, convw_ref, sinks_ref, ga_ref, gc_ref,
                  w_out_ref, g2_ref, w1_ref, w2_ref, gf_ref,
                  o_ref,
                  bias_scr, q_scr, k_scr, vt_scr, attn_scr, gates_scr, cu_scr, mixed_scr,
                  y_scr, z_scr, a_scr):
    step = pl.program_id(0)
    t = lax.rem(jnp.minimum(step, total - 1), n_tiles)
    bf16 = jnp.bfloat16
    f32 = jnp.float32

    col = lax.broadcasted_iota(jnp.int32, (1, GROUP * BLOCK), 1)
    head_col = (col // HALF) % GROUP
    qcol = (col // (GROUP * HALF)) * HALF + col % HALF
    slot_cols = lax.broadcasted_iota(jnp.int32, (BLOCK, GROUP * BLOCK), 0)
    cur_cols = slot_cols <= qcol
    tri = (lax.broadcasted_iota(jnp.int32, (HALF, GROUP * HALF), 0)
           <= lax.broadcasted_iota(jnp.int32, (HALF, GROUP * HALF), 1) % HALF)
    low_half = lax.broadcasted_iota(jnp.int32, (HEAD_DIM, LANES), 1) < HALF

    @pl.when(step == 0)
    def _first_step():
        mixed_scr[...] = jnp.zeros((TILE, D_MODEL), bf16)

    @pl.when(t == 0)
    def _start_of_sequence():
        k_scr[:, 0:BLOCK, :] = jnp.zeros((KV_HEADS, BLOCK, LANES), bf16)
        vt_scr[:, 0:BLOCK] = jnp.zeros((KV_WIDTH, BLOCK), bf16)
        cu_scr[0:SUBLANES, :] = jnp.zeros((SUBLANES, CONV_WIDTH), f32)
        dist = jnp.where(cur_cols, qcol - slot_cols, qcol - slot_cols + BLOCK).astype(f32)
        for g in range(KV_HEADS):
            slope = jnp.zeros((1, GROUP * BLOCK), f32)
            for hh in range(GROUP):
                slope = jnp.where(head_col == hh, 2.0 ** (-8.0 * (GROUP * g + hh + 1) / ATTN_HEADS), slope)
            b = -slope * dist
            bias_scr[0, g] = b
            bias_scr[1, g] = jnp.where(cur_cols, b, MASKED)

    first = jnp.where(t == 0, 1, 0)
    low = lax.broadcasted_iota(jnp.int32, (CHAIN, LANES), 1) < HEAD_DIM
    zero = jnp.zeros((CHAIN, LANES), f32)
    gate_tiles = list(range(OFF_C, IN_WIDTH, GATE_TILE))

    def front(c):
        r0 = c * CHAIN
        rows = slice(r0, r0 + CHAIN)
        blocks = range(c * CHAIN_BLOCKS, (c + 1) * CHAIN_BLOCKS)

        x = x_ref[rows, :]
        y_scr[rows, :] = _rms(x, g1_ref[...]).astype(bf16)
        yield

        q = _dot(y_scr[rows, :], w_in_ref[:, OFF_Q:OFF_Q + ATTN_WIDTH]) * (HEAD_DIM ** -0.5)
        for pair in range(ATTN_HEADS // 2):
            qp = q[:, pair * LANES:(pair + 1) * LANES]
            g, hh = divmod(2 * pair, GROUP)
            for r, masked in enumerate((jnp.where(low, qp, zero), jnp.where(low, zero, qp))):
                masked = masked.astype(bf16)
                for jj, j in enumerate(blocks):
                    for half in range(2):
                        q0 = jj * BLOCK + half * HALF
                        q_scr[j, g, half, hh + r] = masked[q0:q0 + HALF]
        kv = _dot(y_scr[rows, :], w_in_ref[:, OFF_KV:OFF_KV + 2 * KV_WIDTH])
        k = kv[:, :KV_WIDTH]
        k_swapped = pltpu.roll(k, HEAD_DIM, axis=1)
        k_scr[0, BLOCK + r0:BLOCK + r0 + CHAIN, :] = jnp.where(low, k, k_swapped).astype(bf16)
        k_scr[1, BLOCK + r0:BLOCK + r0 + CHAIN, :] = jnp.where(low, k_swapped, k).astype(bf16)
        vt_scr[:, BLOCK + r0:BLOCK + r0 + CHAIN] = kv[:, KV_WIDTH:].T.astype(bf16)
        yield

        def gate_piece(i):
            if i < len(gate_tiles):
                c0 = gate_tiles[i]
                gates_scr[rows, c0 - OFF_C:c0 - OFF_C + GATE_TILE] = _dot(
                    y_scr[rows, :], w_in_ref[:, c0:c0 + GATE_TILE])

        def scores(j, g):
            band = k_scr[g, j * BLOCK:(j + 2) * BLOCK, :]
            nt = (((1,), (1,)), ((), ()))
            q_a = q_scr[j, g, 0].reshape(GROUP * HALF, LANES)
            q_b = q_scr[j, g, 1].reshape(GROUP * HALF, LANES)
            s_a = lax.dot_general(band[:BLOCK + HALF], q_a, nt, preferred_element_type=f32)
            s_b = lax.dot_general(band[HALF:], q_b, nt, preferred_element_type=f32)
            return s_a, s_b

        def softmax(j, g, st):
            s_a, s_b = st
            table = first if j == 0 else 0
            logits = jnp.concatenate([
                jnp.concatenate([jnp.where(tri, s_a[BLOCK:], s_a[:HALF]), s_a[HALF:BLOCK]], axis=0),
                jnp.concatenate([s_b[HALF:BLOCK], jnp.where(tri, s_b[BLOCK:], s_b[:HALF])], axis=0),
            ], axis=1) + bias_scr[table, g]
            sink = jnp.zeros((1, GROUP * BLOCK), f32)
            for hh in range(GROUP):
                sink = jnp.where(head_col == hh, sinks_ref[GROUP * g + hh], sink)
            m = jnp.maximum(jnp.max(logits, axis=0, keepdims=True), sink)
            p = jnp.exp(logits - m)
            den = jnp.sum(p, axis=0, keepdims=True) + jnp.exp(sink - m)
            p_a, p_b = p[:, :GROUP * HALF], p[:, GROUP * HALF:]
            zeros = jnp.zeros((HALF, GROUP * HALF), f32)
            pt_a = jnp.concatenate([jnp.where(tri, zeros, p_a[:HALF]), p_a[HALF:],
                                    jnp.where(tri, p_a[:HALF], zeros)], axis=0).astype(bf16)
            pt_b = jnp.concatenate([jnp.where(tri, zeros, p_b[HALF:]), p_b[:HALF],
                                    jnp.where(tri, p_b[HALF:], zeros)], axis=0).astype(bf16)
            return pt_a, pt_b, 1.0 / den

        def values(j, g, pt_a, pt_b, inv):
            vt = vt_scr[g * HEAD_DIM:(g + 1) * HEAD_DIM, j * BLOCK:(j + 2) * BLOCK]
            o_a = _dot(vt[:, :BLOCK + HALF], pt_a) * inv[:, :GROUP * HALF]
            o_b = _dot(vt[:, HALF:], pt_b) * inv[:, GROUP * HALF:]
            for hp in range(GROUP // 2):
                a0 = o_a[:, hp * LANES:(hp + 1) * LANES]
                a1 = o_b[:, hp * LANES:(hp + 1) * LANES]
                even = jnp.where(low_half, a0, pltpu.roll(a1, HALF, axis=1))
                odd = jnp.where(low_half, pltpu.roll(a0, HALF, axis=1), a1)
                pair = (GROUP // 2) * g + hp
                attn_scr[j * BLOCK:(j + 1) * BLOCK, pair * LANES:(pair + 1) * LANES] = (
                    jnp.concatenate([even, odd], axis=0).T)

        units = [(j, g) for j in blocks for g in range(KV_HEADS)]
        assert len(gate_tiles) <= len(units) + 2
        gate_piece(0)
        st = {0: scores(*units[0]), 1: scores(*units[1])}
        sm = {}
        for i, (j, g) in enumerate(units):
            sm[i] = softmax(j, g, st.pop(i))
            if i + 2 < len(units):
                st[i + 2] = scores(*units[i + 2])
            gate_piece(i + 1)
            if i >= 1:
                values(*units[i - 1], *sm.pop(i - 1))
            if i + 1 == len(units):
                gate_piece(i + 2)
                values(*units[i], *sm.pop(i))
            yield

        conv_pieces = []
        for p0 in range(0, CONV_WIDTH, CONV_PIECE):
            cols = slice(p0, p0 + CONV_PIECE)
            cu = (gates_scr[rows, OFF_C - OFF_C + p0:OFF_C - OFF_C + p0 + CONV_PIECE]
                  * gates_scr[rows, OFF_U - OFF_C + p0:OFF_U - OFF_C + p0 + CONV_PIECE])
            cu_scr[SUBLANES + r0:SUBLANES + r0 + CHAIN, cols] = cu
            cw = convw_ref[:, cols]
            ext = jnp.concatenate([cu_scr[r0:r0 + SUBLANES, cols], cu], axis=0)
            prev1 = pltpu.roll(ext, 1, axis=0)[SUBLANES:]
            prev2 = pltpu.roll(ext, 2, axis=0)[SUBLANES:]
            conv = cw[0:1, :] * prev2 + cw[1:2, :] * prev1 + cw[2:3, :] * cu
            conv_pieces.append(
                gates_scr[rows, OFF_B - OFF_C + p0:OFF_B - OFF_C + p0 + CONV_PIECE] * conv)
            if p0 + CONV_PIECE < CONV_WIDTH:
                yield
        conv = jnp.concatenate(conv_pieces, axis=1)
        mixed_scr[rows, ATTN_WIDTH:] = _rms(conv, gc_ref[...]).astype(bf16)
        mixed_scr[rows, :ATTN_WIDTH] = _rms(attn_scr[rows, :], ga_ref[...]).astype(bf16)

    def back(c):
        r0 = c * CHAIN
        rows = slice(r0, r0 + CHAIN)

        o_ref[rows, :] = xp_ref[rows, :] + _dot(mixed_scr[rows, :], w_out_ref[...])
        yield

        z_scr[rows, :] = _rms(o_ref[rows, :], g2_ref[...]).astype(bf16)
        yield
        for cc in range(D_FF // FF_CHUNK):
            a = _dot(z_scr[rows, :], w1_ref[:, cc * FF_CHUNK:(cc + 1) * FF_CHUNK])
            a_scr[rows, :] = jnp.square(jnp.maximum(a, 0.0)).astype(bf16)
            yield
            o_ref[rows, :] += _dot(a_scr[rows, :], w2_ref[cc * FF_CHUNK:(cc + 1) * FF_CHUNK, :])
            yield

        o_ref[rows, :] = _rms(o_ref[rows, :], gf_ref[...])

    chains = {"A": front(0), "B": front(1), "C": back(0), "D": back(1)}
    finished = set()
    for name in _EMISSION:
        assert name not in finished
        if next(chains[name], "done") == "done":
            finished.add(name)
    assert finished == set(chains), "emission plan does not cover every stage"

    k_scr[:, 0:BLOCK, :] = k_scr[:, TILE:TILE + BLOCK, :]
    vt_scr[:, 0:BLOCK] = vt_scr[:, TILE:TILE + BLOCK]
    cu_scr[0:SUBLANES, :] = cu_scr[TILE:TILE + SUBLANES, :]


def _resident(shape):
    return pl.BlockSpec(shape, lambda s: (0,) * len(shape),
                        pipeline_mode=pl.Buffered(1))


@jax.jit
def kernel(x, norm1_g, w_in, conv_w, sinks, attn_norm_g, conv_norm_g, w_out, norm2_g,
           w_ff1, w_ff2, final_g):
    batch, seq, d = x.shape
    assert d == D_MODEL and seq % TILE == 0
    assert norm1_g.shape[0] == 1, "single layer"
    bf16 = jnp.bfloat16
    row = lambda v: v.reshape(1, -1).astype(jnp.float32)

    n_tiles = seq // TILE
    total = batch * n_tiles

    def tile_at(offset):
        def index(s):
            i = jnp.clip(s + offset, 0, total - 1)
            return (i // n_tiles, i % n_tiles, 0)
        return pl.BlockSpec((None, TILE, D_MODEL), index)

    in_specs = [
        tile_at(0),
        tile_at(-1),
        _resident((1, D_MODEL)),
        _resident((D_MODEL, IN_WIDTH)),
        _resident((CONV_K, CONV_WIDTH)),
        pl.BlockSpec(memory_space=pltpu.SMEM),
        _resident((1, ATTN_WIDTH)),
        _resident((1, CONV_WIDTH)),
        _resident((D_MODEL, D_MODEL)),
        _resident((1, D_MODEL)),
        _resident((D_MODEL, D_FF)),
        _resident((D_FF, D_MODEL)),
        _resident((1, D_MODEL)),
    ]
    scratch_shapes = [
        pltpu.VMEM((2, KV_HEADS, BLOCK, GROUP * BLOCK), jnp.float32),
        pltpu.VMEM((TILE // BLOCK, KV_HEADS, 2, GROUP, HALF, LANES), bf16),
        pltpu.VMEM((KV_HEADS, BLOCK + TILE, LANES), bf16),
        pltpu.VMEM((KV_WIDTH, BLOCK + TILE), bf16),
        pltpu.VMEM((TILE,ATTN_WIDTH), jnp.float32),
        pltpu.VMEM((TILE,3 * CONV_WIDTH), jnp.float32),
        pltpu.VMEM((SUBLANES + TILE, CONV_WIDTH), jnp.float32),
        pltpu.VMEM((TILE,D_MODEL), bf16),
        pltpu.VMEM((TILE,D_MODEL), bf16),
        pltpu.VMEM((TILE,D_MODEL), bf16),
        pltpu.VMEM((TILE,FF_CHUNK), bf16),
    ]
    return pl.pallas_call(
        functools.partial(_layer_kernel, n_tiles, total),
        grid=(total + 1,),
        in_specs=in_specs,
        out_specs=tile_at(-1),
        out_shape=jax.ShapeDtypeStruct(x.shape, x.dtype),
        scratch_shapes=scratch_shapes,
        compiler_params=pltpu.CompilerParams(
            dimension_semantics=("arbitrary",),
            vmem_limit_bytes=VMEM_LIMIT_BYTES),
        name="hybrid_layer",
    )(x, x, row(norm1_g[0]), w_in[0].astype(bf16), conv_w[0].astype(jnp.float32),
      sinks[0].astype(jnp.float32), row(attn_norm_g[0]), row(conv_norm_g[0]),
      w_out[0].astype(bf16), row(norm2_g[0]), w_ff1[0].astype(bf16),
      w_ff2[0].astype(bf16), row(final_g))
```

```python
import jax
import jax.numpy as jnp
from jax import lax
from jax.experimental import pallas as pl
from jax.experimental.pallas import tpu as pltpu

D_MODEL = 1024
ATTN_HEADS = 8
KV_HEADS = 2
HEAD_DIM = 64
ATTN_WIDTH = ATTN_HEADS * HEAD_DIM
KV_WIDTH = KV_HEADS * HEAD_DIM
GROUP = ATTN_HEADS // KV_HEADS
WINDOW = 128
BLOCK = 128
HALF = BLOCK // 2
CONV_WIDTH = D_MODEL - ATTN_WIDTH
CONV_K = 3
D_FF = 4 * D_MODEL
EPS = 1e-5
OFF_Q = 0
OFF_KV = ATTN_WIDTH
OFF_C = ATTN_WIDTH + 2 * KV_WIDTH
OFF_B = OFF_C + CONV_WIDTH
OFF_U = OFF_B + CONV_WIDTH
IN_WIDTH = OFF_U + CONV_WIDTH

LANES = 128
SUBLANES = 8
TILE = 512
N_CHAINS = 2
CHAIN = TILE // N_CHAINS
CHAIN_BLOCKS = CHAIN // BLOCK
FF_CHUNK = 1024
GATE_TILE = 256
ROW_G1, ROW_G2, ROW_GF, ROW_GMIX, ROW_CONV, VEC_ROWS = 0, 1, 2, 3, 4, 8
MASKED = -1e30
VMEM_LIMIT_BYTES = 56 * 1024 * 1024

assert WINDOW == BLOCK and HEAD_DIM * 2 == LANES and KV_WIDTH == LANES

CONV_PIECE = 256
_STAGES_PER_CHAIN = 19
_EMISSION = "AABBAAAABABABABAABABABABABABABBAABBBBB"
assert _EMISSION.count("A") == _EMISSION.count("B") == _STAGES_PER_CHAIN


def _stage_positions(name):
    return [i for i, ch in enumerate(_EMISSION) if ch == name]


for _a, _b in ((1, 2), (6, 6), (7, 7)):
    assert _stage_positions("A")[_a] < _stage_positions("B")[_b]


def _rms(xf, g):
    inv = lax.rsqrt(jnp.mean(xf * xf, axis=-1, keepdims=True) + EPS)
    return xf * inv * g


def _dot(a, b):
    return jnp.dot(a, b, preferred_element_type=jnp.float32)


def _layer_kernel(x_ref, vec_ref, w_in_ref, sinks_ref, w_out_ref, w1_ref, w2_ref,
                  o_ref,
                  bias_scr, q_scr, k_scr, vt_scr, attn_scr, gates_scr, cu_scr, mixed_scr,
                  y_scr, z_scr, a_scr):
    t = pl.program_id(1)
    bf16 = jnp.bfloat16
    f32 = jnp.float32

    col = lax.broadcasted_iota(jnp.int32, (1, GROUP * BLOCK), 1)
    head_col = (col // HALF) % GROUP
    qcol = (col // (GROUP * HALF)) * HALF + col % HALF
    slot_cols = lax.broadcasted_iota(jnp.int32, (BLOCK, GROUP * BLOCK), 0)
    cur_cols = slot_cols <= qcol
    tri = (lax.broadcasted_iota(jnp.int32, (HALF, GROUP * HALF), 0)
           <= lax.broadcasted_iota(jnp.int32, (HALF, GROUP * HALF), 1) % HALF)
    low_half = lax.broadcasted_iota(jnp.int32, (HEAD_DIM, LANES), 1) < HALF

    @pl.when(t == 0)
    def _start_of_sequence():
        k_scr[:, 0:BLOCK, :] = jnp.zeros((KV_HEADS, BLOCK, LANES), bf16)
        vt_scr[:, 0:BLOCK] = jnp.zeros((KV_WIDTH, BLOCK), bf16)
        cu_scr[0:SUBLANES, :] = jnp.zeros((SUBLANES, CONV_WIDTH), f32)
        dist = jnp.where(cur_cols, qcol - slot_cols, qcol - slot_cols + BLOCK).astype(f32)
        for g in range(KV_HEADS):
            slope = jnp.zeros((1, GROUP * BLOCK), f32)
            for hh in range(GROUP):
                slope = jnp.where(head_col == hh, 2.0 ** (-8.0 * (GROUP * g + hh + 1) / ATTN_HEADS), slope)
            b = -slope * dist
            bias_scr[0, g] = b
            bias_scr[1, g] = jnp.where(cur_cols, b, MASKED)

    first = jnp.where(t == 0, 1, 0)
    low = lax.broadcasted_iota(jnp.int32, (CHAIN, LANES), 1) < HEAD_DIM
    zero = jnp.zeros((CHAIN, LANES), f32)
    gate_tiles = list(range(OFF_C, IN_WIDTH, GATE_TILE))

    def chain(c):
        r0 = c * CHAIN
        rows = slice(r0, r0 + CHAIN)
        blocks = range(c * CHAIN_BLOCKS, (c + 1) * CHAIN_BLOCKS)

        x = x_ref[rows, :]
        y_scr[rows, :] = _rms(x, vec_ref[ROW_G1:ROW_G1 + 1, :]).astype(bf16)
        yield

        q = _dot(y_scr[rows, :], w_in_ref[:, OFF_Q:OFF_Q + ATTN_WIDTH]) * (HEAD_DIM ** -0.5)
        for pair in range(ATTN_HEADS // 2):
            qp = q[:, pair * LANES:(pair + 1) * LANES]
            g, hh = divmod(2 * pair, GROUP)
            for r, masked in enumerate((jnp.where(low, qp, zero), jnp.where(low, zero, qp))):
                masked = masked.astype(bf16)
                for jj, j in enumerate(blocks):
                    for half in range(2):
                        q0 = jj * BLOCK + half * HALF
                        q_scr[j, g, half, hh + r] = masked[q0:q0 + HALF]
        kv = _dot(y_scr[rows, :], w_in_ref[:, OFF_KV:OFF_KV + 2 * KV_WIDTH])
        k = kv[:, :KV_WIDTH]
        k_swapped = pltpu.roll(k, HEAD_DIM, axis=1)
        k_scr[0, BLOCK + r0:BLOCK + r0 + CHAIN, :] = jnp.where(low, k, k_swapped).astype(bf16)
        k_scr[1, BLOCK + r0:BLOCK + r0 + CHAIN, :] = jnp.where(low, k_swapped, k).astype(bf16)
        vt_scr[:, BLOCK + r0:BLOCK + r0 + CHAIN] = kv[:, KV_WIDTH:].T.astype(bf16)
        yield

        def gate_piece(i):
            if i < len(gate_tiles):
                c0 = gate_tiles[i]
                gates_scr[rows, c0 - OFF_C:c0 - OFF_C + GATE_TILE] = _dot(
                    y_scr[rows, :], w_in_ref[:, c0:c0 + GATE_TILE])

        def scores(j, g):
            band = k_scr[g, j * BLOCK:(j + 2) * BLOCK, :]
            nt = (((1,), (1,)), ((), ()))
            q_a = q_scr[j, g, 0].reshape(GROUP * HALF, LANES)
            q_b = q_scr[j, g, 1].reshape(GROUP * HALF, LANES)
            s_a = lax.dot_general(band[:BLOCK + HALF], q_a, nt, preferred_element_type=f32)
            s_b = lax.dot_general(band[HALF:], q_b, nt, preferred_element_type=f32)
            return s_a, s_b

        def softmax(j, g, st):
            s_a, s_b = st
            table = first if j == 0 else 0
            logits = jnp.concatenate([
                jnp.concatenate([jnp.where(tri, s_a[BLOCK:], s_a[:HALF]), s_a[HALF:BLOCK]], axis=0),
                jnp.concatenate([s_b[HALF:BLOCK], jnp.where(tri, s_b[BLOCK:], s_b[:HALF])], axis=0),
            ], axis=1) + bias_scr[table, g]
            sink = jnp.zeros((1, GROUP * BLOCK), f32)
            for hh in range(GROUP):
                sink = jnp.where(head_col == hh, sinks_ref[GROUP * g + hh], sink)
            m = jnp.maximum(jnp.max(logits, axis=0, keepdims=True), sink)
            p = jnp.exp(logits - m)
            den = jnp.sum(p, axis=0, keepdims=True) + jnp.exp(sink - m)
            p_a, p_b = p[:, :GROUP * HALF], p[:, GROUP * HALF:]
            zeros = jnp.zeros((HALF, GROUP * HALF), f32)
            pt_a = jnp.concatenate([jnp.where(tri, zeros, p_a[:HALF]), p_a[HALF:],
                                    jnp.where(tri, p_a[:HALF], zeros)], axis=0).astype(bf16)
            pt_b = jnp.concatenate([jnp.where(tri, zeros, p_b[HALF:]), p_b[:HALF],
                                    jnp.where(tri, p_b[HALF:], zeros)], axis=0).astype(bf16)
            return pt_a, pt_b, 1.0 / den

        def values(j, g, pt_a, pt_b, inv):
            vt = vt_scr[g * HEAD_DIM:(g + 1) * HEAD_DIM, j * BLOCK:(j + 2) * BLOCK]
            o_a = _dot(vt[:, :BLOCK + HALF], pt_a) * inv[:, :GROUP * HALF]
            o_b = _dot(vt[:, HALF:], pt_b) * inv[:, GROUP * HALF:]
            for hp in range(GROUP // 2):
                a0 = o_a[:, hp * LANES:(hp + 1) * LANES]
                a1 = o_b[:, hp * LANES:(hp + 1) * LANES]
                even = jnp.where(low_half, a0, pltpu.roll(a1, HALF, axis=1))
                odd = jnp.where(low_half, pltpu.roll(a0, HALF, axis=1), a1)
                pair = (GROUP // 2) * g + hp
                attn_scr[j * BLOCK:(j + 1) * BLOCK, pair * LANES:(pair + 1) * LANES] = (
                    jnp.concatenate([even, odd], axis=0).T)

        units = [(j, g) for j in blocks for g in range(KV_HEADS)]
        assert len(gate_tiles) <= len(units) + 2
        gate_piece(0)
        st = {0: scores(*units[0]), 1: scores(*units[1])}
        sm = {}
        for i, (j, g) in enumerate(units):
            sm[i] = softmax(j, g, st.pop(i))
            if i + 2 < len(units):
                st[i + 2] = scores(*units[i + 2])
            gate_piece(i + 1)
            if i >= 1:
                values(*units[i - 1], *sm.pop(i - 1))
            if i + 1 == len(units):
                gate_piece(i + 2)
                values(*units[i], *sm.pop(i))
            yield

        conv_pieces = []
        for p0 in range(0, CONV_WIDTH, CONV_PIECE):
            cols = slice(p0, p0 + CONV_PIECE)
            cu = (gates_scr[rows, OFF_C - OFF_C + p0:OFF_C - OFF_C + p0 + CONV_PIECE]
                  * gates_scr[rows, OFF_U - OFF_C + p0:OFF_U - OFF_C + p0 + CONV_PIECE])
            cu_scr[SUBLANES + r0:SUBLANES + r0 + CHAIN, cols] = cu
            cw = vec_ref[ROW_CONV:ROW_CONV + CONV_K, cols]
            ext = jnp.concatenate([cu_scr[r0:r0 + SUBLANES, cols], cu], axis=0)
            prev1 = pltpu.roll(ext, 1, axis=0)[SUBLANES:]
            prev2 = pltpu.roll(ext, 2, axis=0)[SUBLANES:]
            conv = cw[0:1, :] * prev2 + cw[1:2, :] * prev1 + cw[2:3, :] * cu
            conv_pieces.append(
                gates_scr[rows, OFF_B - OFF_C + p0:OFF_B - OFF_C + p0 + CONV_PIECE] * conv)
            if p0 + CONV_PIECE < CONV_WIDTH:
                yield
        conv = jnp.concatenate(conv_pieces, axis=1)
        mixed_scr[rows, ATTN_WIDTH:] = _rms(conv, vec_ref[ROW_GMIX:ROW_GMIX + 1, ATTN_WIDTH:]).astype(bf16)
        mixed_scr[rows, :ATTN_WIDTH] = _rms(attn_scr[rows, :], vec_ref[ROW_GMIX:ROW_GMIX + 1, :ATTN_WIDTH]).astype(bf16)
        yield

        o_ref[rows, :] = x + _dot(mixed_scr[rows, :], w_out_ref[...])
        yield

        z_scr[rows, :] = _rms(o_ref[rows, :], vec_ref[ROW_G2:ROW_G2 + 1, :]).astype(bf16)
        yield
        for cc in range(D_FF // FF_CHUNK):
            a = _dot(z_scr[rows, :], w1_ref[:, cc * FF_CHUNK:(cc + 1) * FF_CHUNK])
            a_scr[rows, :] = jnp.square(jnp.maximum(a, 0.0)).astype(bf16)
            yield
            o_ref[rows, :] += _dot(a_scr[rows, :], w2_ref[cc * FF_CHUNK:(cc + 1) * FF_CHUNK, :])
            yield

        o_ref[rows, :] = _rms(o_ref[rows, :], vec_ref[ROW_GF:ROW_GF + 1, :])

    chains = {"A": chain(0), "B": chain(1)}
    finished = set()
    for name in _EMISSION:
        assert name not in finished
        if next(chains[name], "done") == "done":
            finished.add(name)
    assert finished == set(chains), "emission plan does not cover every stage"

    k_scr[:, 0:BLOCK, :] = k_scr[:, TILE:TILE + BLOCK, :]
    vt_scr[:, 0:BLOCK] = vt_scr[:, TILE:TILE + BLOCK]
    cu_scr[0:SUBLANES, :] = cu_scr[TILE:TILE + SUBLANES, :]


def _resident(shape):
    return pl.BlockSpec(shape, lambda b, t: (0,) * len(shape),
                        pipeline_mode=pl.Buffered(1))


@jax.jit
def kernel(x, norm1_g, w_in, conv_w, sinks, attn_norm_g, conv_norm_g, w_out, norm2_g,
           w_ff1, w_ff2, final_g):
    batch, seq, d = x.shape
    assert d == D_MODEL and seq % TILE == 0
    assert norm1_g.shape[0] == 1, "single layer"
    bf16 = jnp.bfloat16
    f32 = jnp.float32
    vecs = jnp.concatenate([
        norm1_g[0].reshape(1, -1), norm2_g[0].reshape(1, -1), final_g.reshape(1, -1),
        jnp.concatenate([attn_norm_g[0], conv_norm_g[0]]).reshape(1, -1),
        jnp.pad(conv_w[0], ((0, 0), (0, D_MODEL - CONV_WIDTH))),
        jnp.zeros((VEC_ROWS - ROW_CONV - CONV_K, D_MODEL), conv_w.dtype)], axis=0).astype(f32)

    tile_spec = pl.BlockSpec((None, TILE, D_MODEL), lambda b, t: (b, t, 0))
    in_specs = [
        tile_spec,
        _resident((VEC_ROWS, D_MODEL)),
        _resident((D_MODEL, IN_WIDTH)),
        pl.BlockSpec(memory_space=pltpu.SMEM),
        _resident((D_MODEL, D_MODEL)),
        _resident((D_MODEL, D_FF)),
        _resident((D_FF, D_MODEL)),
    ]
    scratch_shapes = [
        pltpu.VMEM((2, KV_HEADS, BLOCK, GROUP * BLOCK), jnp.float32),
        pltpu.VMEM((TILE // BLOCK, KV_HEADS, 2, GROUP, HALF, LANES), bf16),
        pltpu.VMEM((KV_HEADS, BLOCK + TILE, LANES), bf16),
        pltpu.VMEM((KV_WIDTH, BLOCK + TILE), bf16),
        pltpu.VMEM((TILE,ATTN_WIDTH), jnp.float32),
        pltpu.VMEM((TILE,3 * CONV_WIDTH), jnp.float32),
        pltpu.VMEM((SUBLANES + TILE, CONV_WIDTH), jnp.float32),
        pltpu.VMEM((TILE,D_MODEL), bf16),
        pltpu.VMEM((TILE,D_MODEL), bf16),
        pltpu.VMEM((TILE,D_MODEL), bf16),
        pltpu.VMEM((TILE,FF_CHUNK), bf16),
    ]
    return pl.pallas_call(
        _layer_kernel,
        grid=(batch, seq // TILE),
        in_specs=in_specs,
        out_specs=tile_spec,
        out_shape=jax.ShapeDtypeStruct(x.shape, x.dtype),
        scratch_shapes=scratch_shapes,
        compiler_params=pltpu.CompilerParams(
            dimension_semantics=("arbitrary", "arbitrary"),
            vmem_limit_bytes=VMEM_LIMIT_BYTES),
        name="hybrid_layer",
    )(x, vecs, w_in[0].astype(bf16), sinks[0].astype(f32), w_out[0].astype(bf16),
      w_ff1[0].astype(bf16), w_ff2[0].astype(bf16))
```
